```python
import jax, jax.numpy as jnp
from jax import lax
import numpy as np

D_MODEL = 2048
BATCH = 4
SEQ = 2048
DEPTH = 4
DEC_BATCH = 32
DEC_SEQ = 1
PAST_LEN = 16384
PAGE_SIZE = 128

N_MIXERS = 2
N_A_LAYERS = (DEPTH + 1) // N_MIXERS
N_B_LAYERS = DEPTH // N_MIXERS
A_HEADS = 16
A_KV_HEADS = 8
A_HEAD_DIM = 128
A_GROUPS = A_HEADS // A_KV_HEADS
A_WIDTH = A_HEADS * A_HEAD_DIM
A_KV_WIDTH = A_KV_HEADS * A_HEAD_DIM
A_IN = 2 * A_WIDTH + 2 * A_KV_WIDTH
A_BIAS_INIT = -5.0
B_HEADS = 32
B_KV_HEADS = 4
B_HEAD_DIM = 64
B_GROUPS = B_HEADS // B_KV_HEADS
B_WIDTH = B_HEADS * B_HEAD_DIM
B_KV_WIDTH = B_KV_HEADS * B_HEAD_DIM
B_IN = 2 * B_WIDTH + 2 * B_KV_WIDTH
WINDOW = 128
Q_BLOCK = 128
ROPE_THETA = 10000.0
EPS = 1e-6

kernel_name = 'hybrid_stickbreak_swa_sink_decode_step'


def _rmsnorm(x, g):
    xf = x.astype(jnp.float32)
    xf = xf * lax.rsqrt(jnp.mean(xf * xf, axis=-1, keepdims=True) + EPS)
    return (xf * g.astype(jnp.float32)).astype(x.dtype)


def _rope(x, pos):
    half = x.shape[-1] // 2
    inv = ROPE_THETA ** (-(jnp.arange(half, dtype=jnp.float32) * 2.0 / x.shape[-1]))
    ang = pos.astype(jnp.float32)[:, None] * inv[None, :]
    cos, sin = jnp.cos(ang)[:, None, :], jnp.sin(ang)[:, None, :]
    x1, x2 = x[..., :half], x[..., half:]
    return jnp.concatenate([x1 * cos - x2 * sin, x2 * cos + x1 * sin], axis=-1)


def _stick_break(z, mask, suffix):
    p = jax.nn.softplus(z)
    if mask is not None:
        p = jnp.where(mask, p, 0.0)
    s_excl = suffix[..., None] + lax.cumsum(p, axis=p.ndim - 1, reverse=True) - p
    a = jnp.exp(jax.nn.log_sigmoid(z) - s_excl)
    if mask is not None:
        a = jnp.where(mask, a, 0.0)
    return a, jnp.sum(p, axis=-1)


def _sink_softmax(s, valid, sink):
    s = jnp.where(valid, s, -jnp.inf)
    m = jnp.maximum(jnp.max(s, axis=-1, keepdims=True), sink)
    e = jnp.exp(s - m)
    return e / (jnp.sum(e, axis=-1, keepdims=True) + jnp.exp(sink - m))


def _gated_out(x, o, gate, w_out):
    o2 = o.reshape(x.shape[0], x.shape[1], -1)
    return x + ((o2 * jax.nn.silu(gate)).astype(x.dtype) @ w_out)


def _sb_project(x, g, w_in):
    h = _rmsnorm(x, g)
    proj = (h @ w_in).astype(jnp.float32)
    q, k, v, gate = jnp.split(proj, [A_WIDTH, A_WIDTH + A_KV_WIDTH, A_WIDTH + 2 * A_KV_WIDTH], axis=-1)
    bsz, t = x.shape[0], x.shape[1]
    q = q.reshape(bsz, t, A_KV_HEADS, A_GROUPS, A_HEAD_DIM)
    k = k.reshape(bsz, t, A_KV_HEADS, A_HEAD_DIM)
    v = v.reshape(bsz, t, A_KV_HEADS, A_HEAD_DIM)
    return q, k, v, gate


def _sb_prompt_attn(q, k, v, bias):
    seq = q.shape[1]
    scale = A_HEAD_DIM ** -0.5
    b_h = bias.astype(jnp.float32).reshape(1, A_KV_HEADS, A_GROUPS, 1, 1)
    outs = []
    for b in range(seq // Q_BLOCK):
        lo, hi = b * Q_BLOCK, (b + 1) * Q_BLOCK
        z = jnp.einsum('btkgd,bskd->bkgts', q[:, lo:hi], k[:, :hi]) * scale + b_h
        mask = jnp.arange(hi)[None, :] < (lo + jnp.arange(Q_BLOCK))[:, None]
        a, _ = _stick_break(z, mask, jnp.zeros(z.shape[:-1], jnp.float32))
        outs.append(jnp.einsum('bkgts,bskd->btkgd', a, v[:, :hi]))
    return jnp.concatenate(outs, axis=1)


def _sb_sample_attn(q, k_new, v_new, cache_k, cache_v, layer, page_table, bias):
    scale = A_HEAD_DIM ** -0.5
    b_h = bias.astype(jnp.float32).reshape(1, A_KV_HEADS, A_GROUPS, 1, 1)
    ds = q.shape[1]
    z = jnp.einsum('btkgd,bskd->bkgts', q, k_new) * scale + b_h
    mask = jnp.arange(ds)[None, :] < jnp.arange(ds)[:, None]
    a, acc = _stick_break(z, mask, jnp.zeros(z.shape[:-1], jnp.float32))
    out = jnp.einsum('bkgts,bskd->btkgd', a, v_new)

    def body(carry, pidx):
        out_c, acc_c = carry
        kp = cache_k[layer, pidx].astype(jnp.float32)
        vp = cache_v[layer, pidx].astype(jnp.float32)
        zp = jnp.einsum('btkgd,bskd->bkgts', q, kp) * scale + b_h
        ap, ps = _stick_break(zp, None, acc_c)
        out_c = out_c + jnp.einsum('bkgts,bskd->btkgd', ap, vp)
        return (out_c, acc_c + ps), None

    (out, _), _ = lax.scan(body, (out, acc), page_table.T, reverse=True)
    return out


def _swa_project(x, pos, g, w_in, q_gain, k_gain):
    h = _rmsnorm(x, g)
    proj = (h @ w_in).astype(jnp.float32)
    q, k, v, gate = jnp.split(proj, [B_WIDTH, B_WIDTH + B_KV_WIDTH, B_WIDTH + 2 * B_KV_WIDTH], axis=-1)
    bsz, t = x.shape[0], x.shape[1]
    q = _rope(_rmsnorm(q.reshape(bsz, t, B_HEADS, B_HEAD_DIM), q_gain), pos)
    k = _rope(_rmsnorm(k.reshape(bsz, t, B_KV_HEADS, B_HEAD_DIM), k_gain), pos)
    q = q.reshape(bsz, t, B_KV_HEADS, B_GROUPS, B_HEAD_DIM)
    v = v.reshape(bsz, t, B_KV_HEADS, B_HEAD_DIM)
    return q, k, v, gate


def _swa_prompt_attn(q, k, v, sinks):
    bsz, seq = q.shape[0], q.shape[1]
    nb = seq // WINDOW
    scale = B_HEAD_DIM ** -0.5
    qb = q.reshape(bsz, nb, WINDOW, B_KV_HEADS, B_GROUPS, B_HEAD_DIM)
    kb = k.reshape(bsz, nb, WINDOW, B_KV_HEADS, B_HEAD_DIM)
    vb = v.reshape(bsz, nb, WINDOW, B_KV_HEADS, B_HEAD_DIM)
    pad = ((0, 0), (1, 0), (0, 0), (0, 0), (0, 0))
    kk = jnp.concatenate([jnp.pad(kb, pad)[:, :-1], kb], axis=2)
    vv = jnp.concatenate([jnp.pad(vb, pad)[:, :-1], vb], axis=2)
    s = jnp.einsum('bnqkgd,bnskd->bnkgqs', qb, kk) * scale
    i = jnp.arange(WINDOW)
    j = jnp.arange(2 * WINDOW)
    rel = i[:, None] + WINDOW - j[None, :]
    band = (rel >= 0) & (rel < WINDOW)
    first = (jnp.arange(nb) > 0)[:, None, None] | (j >= WINDOW)[None, None, :]
    valid = (band[None] & first)[None, :, None, None]
    p = _sink_softmax(s, valid, sinks.astype(jnp.float32).reshape(1, 1, B_KV_HEADS, B_GROUPS, 1, 1))
    o = jnp.einsum('bnkgqs,bnskd->bnqkgd', p, vv)
    return o.reshape(bsz, seq, B_KV_HEADS, B_GROUPS, B_HEAD_DIM)


def _swa_sample_attn(q, k_new, v_new, buf_k, buf_v, sinks):
    scale = B_HEAD_DIM ** -0.5
    ds = q.shape[1]
    win = buf_k.shape[1]
    kk = jnp.concatenate([buf_k.astype(jnp.float32), k_new], axis=1)
    vv = jnp.concatenate([buf_v.astype(jnp.float32), v_new], axis=1)
    qpos = PAST_LEN + jnp.arange(ds)
    kpos = jnp.concatenate([PAST_LEN - win + jnp.arange(win), qpos])
    rel = qpos[:, None] - kpos[None, :]
    valid = (rel >= 0) & (rel < WINDOW)
    s = jnp.einsum('btkgd,bskd->bkgts', q, kk) * scale
    p = _sink_softmax(s, valid, sinks.astype(jnp.float32).reshape(1, B_KV_HEADS, B_GROUPS, 1, 1))
    o = jnp.einsum('bkgts,bskd->btkgd', p, vv)
    return o, kk[:, -WINDOW:], vv[:, -WINDOW:]


def setup_inputs(seed: int = 0) -> dict:
    key = jax.random.key(seed)
    ks = jax.random.split(key, 18)
    n_pages = PAST_LEN // PAGE_SIZE
    n_pool = (DEC_BATCH * n_pages * 5) // 4
    f32 = jnp.float32
    page_table = jax.random.permutation(ks[0], n_pool)[:DEC_BATCH * n_pages]
    page_table = page_table.reshape(DEC_BATCH, n_pages).astype(jnp.int32)
    return {
        'x_prompt': jax.random.normal(ks[1], (BATCH, SEQ, D_MODEL), f32),
        'x_sample': jax.random.normal(ks[2], (DEC_BATCH, DEC_SEQ, D_MODEL), f32),
        'cache_sb_k': jax.random.normal(ks[3], (N_A_LAYERS, n_pool, PAGE_SIZE, A_KV_HEADS, A_HEAD_DIM), f32),
        'cache_sb_v': jax.random.normal(ks[4], (N_A_LAYERS, n_pool, PAGE_SIZE, A_KV_HEADS, A_HEAD_DIM), f32),
        'cache_swa_k': jax.random.normal(ks[5], (N_B_LAYERS, DEC_BATCH, WINDOW, B_KV_HEADS, B_HEAD_DIM), f32),
        'cache_swa_v': jax.random.normal(ks[6], (N_B_LAYERS, DEC_BATCH, WINDOW, B_KV_HEADS, B_HEAD_DIM), f32),
        'page_table': page_table,
        'norm_a': 1.0 + 0.02 * jax.random.normal(ks[7], (N_A_LAYERS, D_MODEL), f32),
        'w_in_a': jax.random.normal(ks[8], (N_A_LAYERS, D_MODEL, A_IN), f32) * D_MODEL ** -0.5,
        'w_out_a': jax.random.normal(ks[9], (N_A_LAYERS, A_WIDTH, D_MODEL), f32) * A_WIDTH ** -0.5,
        'sb_bias_a': A_BIAS_INIT + 0.5 * jax.random.normal(ks[16], (N_A_LAYERS, A_HEADS), f32),
        'norm_b': 1.0 + 0.02 * jax.random.normal(ks[10], (N_B_LAYERS, D_MODEL), f32),
        'w_in_b': jax.random.normal(ks[11], (N_B_LAYERS, D_MODEL, B_IN), f32) * D_MODEL ** -0.5,
        'w_out_b': jax.random.normal(ks[12], (N_B_LAYERS, B_WIDTH, D_MODEL), f32) * B_WIDTH ** -0.5,
        'q_norm_b': 1.0 + 0.02 * jax.random.normal(ks[13], (N_B_LAYERS, B_HEAD_DIM), f32),
        'k_norm_b': 1.0 + 0.02 * jax.random.normal(ks[14], (N_B_LAYERS, B_HEAD_DIM), f32),
        'sinks_b': 0.5 * jax.random.normal(ks[15], (N_B_LAYERS, B_HEADS), f32),
    }


def reference(x_prompt, x_sample, cache_sb_k, cache_sb_v, cache_swa_k, cache_swa_v, page_table,
              norm_a, w_in_a, w_out_a, sb_bias_a, norm_b, w_in_b, w_out_b, q_norm_b, k_norm_b, sinks_b):
    pos_p = jnp.arange(x_prompt.shape[1])
    pos_s = PAST_LEN + jnp.arange(x_sample.shape[1])
    yp, ys = x_prompt, x_sample
    sbk_p, sbv_p, sbk_s, sbv_s = [], [], [], []
    swk_p, swv_p, swk_s, swv_s = [], [], [], []
    for i in range(DEPTH):
        j = i // N_MIXERS
        if i % N_MIXERS == 0:
            q, k, v, g = _sb_project(yp, norm_a[j], w_in_a[j])
            yp = _gated_out(yp, _sb_prompt_attn(q, k, v, sb_bias_a[j]), g, w_out_a[j])
            sbk_p.append(k.astype(yp.dtype))
            sbv_p.append(v.astype(yp.dtype))
            q, k, v, g = _sb_project(ys, norm_a[j], w_in_a[j])
            o = _sb_sample_attn(q, k, v, cache_sb_k, cache_sb_v, j, page_table, sb_bias_a[j])
            ys = _gated_out(ys, o, g, w_out_a[j])
            sbk_s.append(k.astype(ys.dtype))
            sbv_s.append(v.astype(ys.dtype))
        else:
            q, k, v, g = _swa_project(yp, pos_p, norm_b[j], w_in_b[j], q_norm_b[j], k_norm_b[j])
            yp = _gated_out(yp, _swa_prompt_attn(q, k, v, sinks_b[j]), g, w_out_b[j])
            swk_p.append(k[:, -WINDOW:].astype(yp.dtype))
            swv_p.append(v[:, -WINDOW:].astype(yp.dtype))
            q, k, v, g = _swa_project(ys, pos_s, norm_b[j], w_in_b[j], q_norm_b[j], k_norm_b[j])
            o, nk, nv = _swa_sample_attn(q, k, v, cache_swa_k[j], cache_swa_v[j], sinks_b[j])
            ys = _gated_out(ys, o, g, w_out_b[j])
            swk_s.append(nk.astype(ys.dtype))
            swv_s.append(nv.astype(ys.dtype))
    return (yp, ys, jnp.stack(sbk_p), jnp.stack(sbv_p), jnp.stack(sbk_s), jnp.stack(sbv_s),
            jnp.stack(swk_p), jnp.stack(swv_p), jnp.stack(swk_s), jnp.stack(swv_s))
```

```python
import functools

import jax
import jax.numpy as jnp
from jax import lax
from jax.experimental import pallas as pl
from jax.experimental.pallas import tpu as pltpu

D_MODEL = 2048
DEPTH = 4
DEC_SEQ = 1
PAST_LEN = 16384
PAGE_SIZE = 128
A_HEADS = 16
A_KV_HEADS = 8
A_HEAD_DIM = 128
A_GROUPS = A_HEADS // A_KV_HEADS
A_WIDTH = A_HEADS * A_HEAD_DIM
A_KV_WIDTH = A_KV_HEADS * A_HEAD_DIM
B_HEADS = 32
B_KV_HEADS = 4
B_HEAD_DIM = 64
B_GROUPS = B_HEADS // B_KV_HEADS
B_WIDTH = B_HEADS * B_HEAD_DIM
B_KV_WIDTH = B_KV_HEADS * B_HEAD_DIM
WINDOW = 128
Q_BLOCK = 128
ROPE_THETA = 10000.0
EPS = 1e-6

LANES = 128
PROJ_TN = 512
VMEM_LIMIT = 56 * 1024 * 1024

F32 = jnp.float32
BF16 = jnp.bfloat16


def _cparams(n_axes):
    return pltpu.CompilerParams(dimension_semantics=("arbitrary",) * n_axes,
                                vmem_limit_bytes=VMEM_LIMIT)


def _nt_dot(a, b):
    return lax.dot_general(a, b, (((1,), (1,)), ((), ())), preferred_element_type=F32)


def _dot(a, b):
    return jnp.dot(a, b, preferred_element_type=F32)


def _split_dot(p, m):
    hi = p.astype(BF16)
    lo = (p - hi.astype(F32)).astype(BF16)
    return _dot(hi, m) + _dot(lo, m)


def _softplus(z):
    return jnp.maximum(z, 0.0) + jnp.log(1.0 + jnp.exp(-jnp.abs(z)))


def _silu(g):
    return g / (1.0 + jnp.exp(-g))


def _rmsnorm_rows(xf, g):
    ms = jnp.mean(xf * xf, axis=-1, keepdims=True)
    return xf * lax.rsqrt(ms + EPS) * g


def _proj_a_kernel(x_ref, g_ref, w_ref, q_ref, k_ref, v_ref, kb_ref, vb_ref, gate_ref, h_scr):
    j = pl.program_id(1)

    @pl.when(j == 0)
    def _():
        h_scr[...] = _rmsnorm_rows(x_ref[...], g_ref[...]).astype(BF16)

    acc = _dot(h_scr[...], w_ref[...])
    nq = A_WIDTH // PROJ_TN
    nk = A_KV_WIDTH // PROJ_TN

    @pl.when(j < nq)
    def _():
        q_ref[...] = (acc * (A_HEAD_DIM ** -0.5)).astype(BF16)

    @pl.when((j >= nq) & (j < nq + nk))
    def _():
        k_ref[...] = acc
        kb_ref[...] = acc.astype(BF16)

    @pl.when((j >= nq + nk) & (j < nq + 2 * nk))
    def _():
        v_ref[...] = acc
        vb_ref[...] = acc.astype(BF16)

    @pl.when(j >= nq + 2 * nk)
    def _():
        gate_ref[...] = acc


def _proj_a(x2d, g, w_bf16, tm):
    m = x2d.shape[0]
    nq = A_WIDTH // PROJ_TN
    nk = A_KV_WIDTH // PROJ_TN
    n_tiles = (2 * A_WIDTH + 2 * A_KV_WIDTH) // PROJ_TN
    grid = (m // tm, n_tiles)

    def clamp(j, lo, n):
        return jnp.clip(j - lo, 0, n - 1)

    return pl.pallas_call(
        _proj_a_kernel,
        grid=grid,
        in_specs=[
            pl.BlockSpec((tm, D_MODEL), lambda i, j: (i, 0)),
            pl.BlockSpec((1, D_MODEL), lambda i, j: (0, 0)),
            pl.BlockSpec((D_MODEL, PROJ_TN), lambda i, j: (0, j)),
        ],
        out_specs=[
            pl.BlockSpec((tm, PROJ_TN), lambda i, j: (i, clamp(j, 0, nq))),
            pl.BlockSpec((tm, PROJ_TN), lambda i, j: (i, clamp(j, nq, nk))),
            pl.BlockSpec((tm, PROJ_TN), lambda i, j: (i, clamp(j, nq + nk, nk))),
            pl.BlockSpec((tm, PROJ_TN), lambda i, j: (i, clamp(j, nq, nk))),
            pl.BlockSpec((tm, PROJ_TN), lambda i, j: (i, clamp(j, nq + nk, nk))),
            pl.BlockSpec((tm, PROJ_TN), lambda i, j: (i, clamp(j, nq + 2 * nk, nq))),
        ],
        out_shape=[
            jax.ShapeDtypeStruct((m, A_WIDTH), BF16),
            jax.ShapeDtypeStruct((m, A_KV_WIDTH), F32),
            jax.ShapeDtypeStruct((m, A_KV_WIDTH), F32),
            jax.ShapeDtypeStruct((m, A_KV_WIDTH), BF16),
            jax.ShapeDtypeStruct((m, A_KV_WIDTH), BF16),
            jax.ShapeDtypeStruct((m, A_WIDTH), F32),
        ],
        scratch_shapes=[pltpu.VMEM((tm, D_MODEL), BF16)],
        compiler_params=_cparams(2),
        name="proj_a",
    )(x2d, g, w_bf16)


B_KV2 = 2 * B_KV_WIDTH


def _head_norm_rope(acc, seg, gain, cos, sin):
    ms = _split_dot(acc * acc, seg) * (1.0 / B_HEAD_DIM)
    xn = acc * lax.rsqrt(ms + EPS) * gain
    half = B_HEAD_DIM // 2
    lane = lax.broadcasted_iota(jnp.int32, (1, LANES), 1)
    first_half = (lane % B_HEAD_DIM) < half
    partners = []
    for c in range(acc.shape[-1] // LANES):
        xc = xn[:, c * LANES:(c + 1) * LANES]
        partners.append(jnp.where(first_half, pltpu.roll(xc, LANES - half, 1), pltpu.roll(xc, half, 1)))
    return xn * cos + jnp.concatenate(partners, axis=1) * sin


def _proj_b_kernel(x_ref, g_ref, w_ref, seg_ref, qg_ref, kg_ref, cos_ref, sin_ref,
                   q_ref, k_ref, v_ref, kb_ref, vb_ref, gate_ref, h_scr):
    j = pl.program_id(1)

    @pl.when(j == 0)
    def _():
        h_scr[...] = _rmsnorm_rows(x_ref[...], g_ref[...]).astype(BF16)

    acc = _dot(h_scr[...], w_ref[...])
    nq = B_WIDTH // PROJ_TN

    @pl.when(j < nq)
    def _():
        qr = _head_norm_rope(acc, seg_ref[...], qg_ref[...], cos_ref[...], sin_ref[...])
        q_ref[...] = (qr * (B_HEAD_DIM ** -0.5)).astype(BF16)

    @pl.when(j == nq)
    def _():
        kr = _head_norm_rope(acc, seg_ref[...], kg_ref[...], cos_ref[...], sin_ref[...])
        k_ref[...] = kr
        kb_ref[...] = kr.astype(BF16)

    @pl.when(j == nq + 1)
    def _():
        v_ref[...] = acc
        vb_ref[...] = acc.astype(BF16)

    @pl.when(j > nq + 1)
    def _():
        gate_ref[...] = acc


def _proj_b(x2d, g, w_bf16, seg, qg, kg, cos_t, sin_t, tm, pos_blocks):
    m = x2d.shape[0]
    assert B_KV2 == PROJ_TN
    nq = B_WIDTH // PROJ_TN
    n_tiles = (2 * B_WIDTH + 2 * B_KV2) // PROJ_TN
    grid = (m // tm, n_tiles)

    def clamp(j, lo, n):
        return jnp.clip(j - lo, 0, n - 1)

    const = lambda i, j: (0, 0)
    return pl.pallas_call(
        _proj_b_kernel,
        grid=grid,
        in_specs=[
            pl.BlockSpec((tm, D_MODEL), lambda i, j: (i, 0)),
            pl.BlockSpec((1, D_MODEL), const),
            pl.BlockSpec((D_MODEL, PROJ_TN), lambda i, j: (0, j)),
            pl.BlockSpec((PROJ_TN, PROJ_TN), const),
            pl.BlockSpec((1, PROJ_TN), const),
            pl.BlockSpec((1, PROJ_TN), const),
            pl.BlockSpec((tm, PROJ_TN), lambda i, j: (i % pos_blocks, 0)),
            pl.BlockSpec((tm, PROJ_TN), lambda i, j: (i % pos_blocks, 0)),
        ],
        out_specs=[
            pl.BlockSpec((tm, PROJ_TN), lambda i, j: (i, clamp(j, 0, nq))),
            pl.BlockSpec((tm, PROJ_TN), lambda i, j: (i, 0)),
            pl.BlockSpec((tm, PROJ_TN), lambda i, j: (i, 0)),
            pl.BlockSpec((tm, PROJ_TN), lambda i, j: (i, 0)),
            pl.BlockSpec((tm, PROJ_TN), lambda i, j: (i, 0)),
            pl.BlockSpec((tm, PROJ_TN), lambda i, j: (i, clamp(j, nq + 2, nq))),
        ],
        out_shape=[
            jax.ShapeDtypeStruct((m, B_WIDTH), BF16),
            jax.ShapeDtypeStruct((m, B_KV2), F32),
            jax.ShapeDtypeStruct((m, B_KV2), F32),
            jax.ShapeDtypeStruct((m, B_KV2), BF16),
            jax.ShapeDtypeStruct((m, B_KV2), BF16),
            jax.ShapeDtypeStruct((m, B_WIDTH), F32),
        ],
        scratch_shapes=[pltpu.VMEM((tm, D_MODEL), BF16)],
        compiler_params=_cparams(2),
        name="proj_b",
    )(x2d, g, w_bf16, seg, qg, kg, cos_t, sin_t)


def _outproj_kernel(og_ref, w_ref, x_ref, y_ref):
    y_ref[...] = x_ref[...] + _dot(og_ref[...], w_ref[...])


def _outproj(og, w_bf16, x2d, tm, tn=1024):
    m, kdim = og.shape
    grid = (m // tm, D_MODEL // tn)
    return pl.pallas_call(
        _outproj_kernel,
        grid=grid,
        in_specs=[
            pl.BlockSpec((tm, kdim), lambda i, j: (i, 0)),
            pl.BlockSpec((kdim, tn), lambda i, j: (0, j)),
            pl.BlockSpec((tm, tn), lambda i, j: (i, j)),
        ],
        out_specs=pl.BlockSpec((tm, tn), lambda i, j: (i, j)),
        out_shape=jax.ShapeDtypeStruct((m, D_MODEL), F32),
        compiler_params=_cparams(2),
        name="outproj",
    )(og, w_bf16, x2d)


def _stick_block(q2, kj, vj, bias, u2, out, accb, mask):
    z = _nt_dot(q2, kj) + bias
    p = _softplus(z)
    if mask is not None:
        p = jnp.where(mask, p, 0.0)
    c2 = _split_dot(p, u2)
    c = c2[:, :LANES]
    tot = c2[:, LANES:]
    a = jnp.exp(z - (accb + c))
    if mask is not None:
        a = jnp.where(mask, a, 0.0)
    out = out + _dot(a.astype(BF16), vj)
    return out, accb + tot


def _attn_a_prompt_kernel(q_ref, k_ref, v_ref, gate_ref, bias_ref, u_ref, og_ref):
    qi = pl.program_id(2)
    hd = A_HEAD_DIM
    rows = A_GROUPS * Q_BLOCK
    q2 = jnp.concatenate([q_ref[0, :, g * hd:(g + 1) * hd] for g in range(A_GROUPS)], axis=0)
    bias = bias_ref[0]
    u2 = u_ref[...]
    t = lax.broadcasted_iota(jnp.int32, (rows, Q_BLOCK), 0) % Q_BLOCK
    s = lax.broadcasted_iota(jnp.int32, (rows, Q_BLOCK), 1)
    mask = s < t

    def kv(j):
        off = pl.multiple_of(j * Q_BLOCK, Q_BLOCK)
        return k_ref[0, pl.ds(off, Q_BLOCK), :], v_ref[0, pl.ds(off, Q_BLOCK), :]

    kd, vd = kv(qi)
    out0 = jnp.zeros((rows, hd), F32)
    acc0 = jnp.zeros((rows, Q_BLOCK), F32)
    out, accb = _stick_block(q2, kd, vd, bias, u2, out0, acc0, mask)

    def body(step, carry):
        kj, vj = kv(qi - 1 - step)
        return _stick_block(q2, kj, vj, bias, u2, carry[0], carry[1], None)

    out, accb = lax.fori_loop(0, qi, body, (out, accb))
    o = jnp.concatenate([out[g * Q_BLOCK:(g + 1) * Q_BLOCK] for g in range(A_GROUPS)], axis=1)
    og_ref[0] = (o * _silu(gate_ref[0])).astype(BF16)


def _attn_a_prompt(q, kb, vb, gate, bias_rows, u2):
    bsz, seq, _ = q.shape
    nq = seq // Q_BLOCK
    gw = A_GROUPS * A_HEAD_DIM
    return pl.pallas_call(
        _attn_a_prompt_kernel,
        grid=(bsz, A_KV_HEADS, nq),
        in_specs=[
            pl.BlockSpec((1, Q_BLOCK, gw), lambda b, h, i: (b, i, h)),
            pl.BlockSpec((1, seq, A_HEAD_DIM), lambda b, h, i: (b, 0, h)),
            pl.BlockSpec((1, seq, A_HEAD_DIM), lambda b, h, i: (b, 0, h)),
            pl.BlockSpec((1, Q_BLOCK, gw), lambda b, h, i: (b, i, h)),
            pl.BlockSpec((1, A_GROUPS * Q_BLOCK, LANES), lambda b, h, i: (h, 0, 0)),
            pl.BlockSpec((Q_BLOCK, 2 * LANES), lambda b, h, i: (0, 0)),
        ],
        out_specs=pl.BlockSpec((1, Q_BLOCK, gw), lambda b, h, i: (b, i, h)),
        out_shape=jax.ShapeDtypeStruct((bsz, seq, A_WIDTH), BF16),
        compiler_params=_cparams(3),
        name="attn_a_prompt",
    )(q, kb, vb, gate, bias_rows, u2)


def _attn_a_sample_kernel(pt_ref, q_ref, knew_ref, vnew_ref, k_ref, v_ref, bias_ref, u_ref, gate_ref,
                          og_ref, qbd_scr, out_scr, acc_scr):
    del pt_ref
    j = pl.program_id(1)
    nh = A_HEADS
    row_kv = lax.broadcasted_iota(jnp.int32, (nh, A_KV_WIDTH), 0) // A_GROUPS
    col_kv = lax.broadcasted_iota(jnp.int32, (nh, A_KV_WIDTH), 1) // A_HEAD_DIM
    bias = bias_ref[...]

    @pl.when(j == 0)
    def _():
        q16 = q_ref[0].astype(F32)
        qbd = jnp.where(row_kv == col_kv, jnp.concatenate([q16] * A_KV_HEADS, axis=1), 0.0)
        qbd_scr[...] = qbd.astype(BF16)
        qb = qbd.astype(BF16).astype(F32)
        kn = knew_ref[0].astype(BF16).astype(F32)
        vn = vnew_ref[0].astype(BF16).astype(F32)
        z = jnp.sum(qb * kn, axis=-1, keepdims=True) + bias[:, :1]
        t_new = DEC_SEQ - 1
        visible = lax.broadcasted_iota(jnp.int32, (nh, DEC_SEQ), 1) < t_new
        sp = _softplus(z)
        p = jnp.where(visible, sp, 0.0)
        a = jnp.where(visible, jnp.exp(z - sp), 0.0)
        out_scr[...] = a * vn
        acc_scr[...] = jnp.broadcast_to(p, (nh, LANES))

    kp = k_ref[0].astype(BF16)
    vp = v_ref[0].astype(BF16)
    z = _nt_dot(qbd_scr[...], kp) + bias
    p = _softplus(z)
    c2 = _split_dot(p, u_ref[...])
    a = jnp.exp(z - (acc_scr[...] + c2[:, :LANES]))
    out_scr[...] += _dot(a.astype(BF16), vp)
    acc_scr[...] += c2[:, LANES:]

    @pl.when(j == pl.num_programs(1) - 1)
    def _():
        full = jnp.where(row_kv == col_kv, out_scr[...], 0.0)
        o = full[:, :A_HEAD_DIM]
        for h in range(1, A_KV_HEADS):
            o = o + full[:, h * A_HEAD_DIM:(h + 1) * A_HEAD_DIM]
        og_ref[0] = (o * _silu(gate_ref[0])).astype(BF16)


def _attn_a_sample(page_table, q3, knew, vnew, kcache, vcache, layer, bias16, u2, gate3):
    nb, n_pages = page_table.shape
    n_pool = kcache.shape[0] // ((DEPTH + 1) // 2)
    pt_flat = page_table.reshape(-1)

    def page_idx(b, j, pt):
        return (layer * n_pool + pt[b * n_pages + (n_pages - 1 - j)], 0, 0)

    per_b = lambda b, j, pt: (b, 0, 0)
    const = lambda b, j, pt: (0, 0)
    grid_spec = pltpu.PrefetchScalarGridSpec(
        num_scalar_prefetch=1,
        grid=(nb, n_pages),
        in_specs=[
            pl.BlockSpec((1, A_HEADS, A_HEAD_DIM), per_b),
            pl.BlockSpec((1, 1, A_KV_WIDTH), per_b),
            pl.BlockSpec((1, 1, A_KV_WIDTH), per_b),
            pl.BlockSpec((1, PAGE_SIZE, A_KV_WIDTH), page_idx),
            pl.BlockSpec((1, PAGE_SIZE, A_KV_WIDTH), page_idx),
            pl.BlockSpec((A_HEADS, LANES), const),
            pl.BlockSpec((PAGE_SIZE, 2 * LANES), const),
            pl.BlockSpec((1, A_HEADS, A_HEAD_DIM), per_b),
        ],
        out_specs=pl.BlockSpec((1, A_HEADS, A_HEAD_DIM), per_b),
        scratch_shapes=[
            pltpu.VMEM((A_HEADS, A_KV_WIDTH), BF16),
            pltpu.VMEM((A_HEADS, A_KV_WIDTH), F32),
            pltpu.VMEM((A_HEADS, LANES), F32),
        ],
    )
    return pl.pallas_call(
        _attn_a_sample_kernel,
        grid_spec=grid_spec,
        out_shape=jax.ShapeDtypeStruct((nb, A_HEADS, A_HEAD_DIM), BF16),
        compiler_params=_cparams(2),
        name="attn_a_sample",
    )(pt_flat, q3, knew, vnew, kcache, vcache, bias16, u2, gate3)


def _sink_softmax(s, sink):
    m = jnp.maximum(jnp.max(s, axis=-1, keepdims=True), sink)
    e = jnp.exp(s - m)
    return e / (jnp.sum(e, axis=-1, keepdims=True) + jnp.exp(sink - m))


def _attn_b_prompt_kernel(q_ref, kp_ref, kc_ref, vp_ref, vc_ref, gate_ref, sink_ref, og_ref):
    n = pl.program_id(1)
    pairs = B_GROUPS // 2
    lane = lax.broadcasted_iota(jnp.int32, (1, LANES), 1)
    low = lane < B_HEAD_DIM
    qt = q_ref[0]
    zero = jnp.zeros((), qt.dtype)
    rows = []
    for c in range(pairs):
        qp = qt[:, c * LANES:(c + 1) * LANES]
        rows += [jnp.where(low, qp, zero), jnp.where(low, zero, qp)]
    q8 = jnp.concatenate(rows, axis=0)
    k2 = jnp.concatenate([kp_ref[0], kc_ref[0]], axis=0)
    v2 = jnp.concatenate([vp_ref[0], vc_ref[0]], axis=0)
    s = _nt_dot(q8, k2)
    nrows = B_GROUPS * WINDOW
    i = lax.broadcasted_iota(jnp.int32, (nrows, 2 * WINDOW), 0) % WINDOW
    jj = lax.broadcasted_iota(jnp.int32, (nrows, 2 * WINDOW), 1)
    rel = i + WINDOW - jj
    valid = (rel >= 0) & (rel < WINDOW) & ((n > 0) | (jj >= WINDOW))
    p = _sink_softmax(jnp.where(valid, s, -jnp.inf), sink_ref[0]).astype(BF16)
    v_lo = jnp.where(low, v2, zero)
    v_hi = jnp.where(low, zero, v2)
    outs = []
    for c in range(pairs):
        pa = p[(2 * c) * WINDOW:(2 * c + 1) * WINDOW]
        pb = p[(2 * c + 1) * WINDOW:(2 * c + 2) * WINDOW]
        outs.append(_dot(pa, v_lo) + _dot(pb, v_hi))
    o = jnp.concatenate(outs, axis=1)
    og_ref[0] = (o * _silu(gate_ref[0])).astype(BF16)


def _attn_b_prompt(q, k2b, v2b, gate, sink_col):
    bsz, seq, _ = q.shape
    nb = seq // WINDOW
    gw = B_GROUPS * B_HEAD_DIM
    prev = lambda b, n, h: (b, jnp.maximum(n - 1, 0), h)
    cur = lambda b, n, h: (b, n, h)
    return pl.pallas_call(
        _attn_b_prompt_kernel,
        grid=(bsz, nb, B_KV_HEADS),
        in_specs=[
            pl.BlockSpec((1, WINDOW, gw), cur),
            pl.BlockSpec((1, WINDOW, LANES), prev),
            pl.BlockSpec((1, WINDOW, LANES), cur),
            pl.BlockSpec((1, WINDOW, LANES), prev),
            pl.BlockSpec((1, WINDOW, LANES), cur),
            pl.BlockSpec((1, WINDOW, gw), cur),
            pl.BlockSpec((1, B_GROUPS * WINDOW, 1), lambda b, n, h: (h, 0, 0)),
        ],
        out_specs=pl.BlockSpec((1, WINDOW, gw), cur),
        out_shape=jax.ShapeDtypeStruct((bsz, seq, B_WIDTH), BF16),
        compiler_params=_cparams(3),
        name="attn_b_prompt",
    )(q, k2b, k2b, v2b, v2b, gate, sink_col)


def _attn_b_sample_kernel(q_ref, knew_ref, vnew_ref, ck_ref, cv_ref, gate_ref, sink_ref,
                          og_ref, nk_ref, nv_ref):
    win = WINDOW
    row = lax.broadcasted_iota(jnp.int32, (win, B_KV_WIDTH), 0)
    newk = jnp.where(row == win - 1, knew_ref[0], pltpu.roll(ck_ref[0], win - 1, 0))
    newv = jnp.where(row == win - 1, vnew_ref[0], pltpu.roll(cv_ref[0], win - 1, 0))
    nk_ref[0] = newk
    nv_ref[0] = newv

    npair = B_HEADS // 2
    lane = lax.broadcasted_iota(jnp.int32, (1, LANES), 1)
    low = lane < B_HEAD_DIM
    q16 = q_ref[0].astype(F32)
    qsw = pltpu.roll(q16, B_HEAD_DIM, 1)
    qa = jnp.where(low, q16, qsw)
    qb = jnp.where(low, qsw, q16)
    q32 = jnp.concatenate([jnp.concatenate([qa, qa], axis=1),
                           jnp.concatenate([qb, qb], axis=1)], axis=0)
    r = lax.broadcasted_iota(jnp.int32, (B_HEADS, B_KV_WIDTH), 0)
    head = jnp.where(r < npair, 2 * r, 2 * (r - npair) + 1)
    own = (head // B_GROUPS) == (lax.broadcasted_iota(jnp.int32, (B_HEADS, B_KV_WIDTH), 1) // B_HEAD_DIM)
    qbd = jnp.where(own, q32, 0.0).astype(BF16)
    s = _nt_dot(qbd, newk.astype(BF16))
    p = _sink_softmax(s, sink_ref[...])
    full = jnp.where(own, _dot(p.astype(BF16), newv.astype(BF16)), 0.0)
    o128 = full[:, :LANES] + full[:, LANES:]
    both = o128 + pltpu.roll(o128, B_HEAD_DIM, 1)
    o16 = jnp.where(low, both[:npair], both[npair:])
    og_ref[0] = (o16 * _silu(gate_ref[0])).astype(BF16)


def _attn_b_sample(q3, knew, vnew, ck, cv, layer, gate3, sink_rows):
    nb = q3.shape[0]
    npair = B_HEADS // 2
    per_b = lambda b: (b, 0, 0)
    cache = lambda b: (layer * nb + b, 0, 0)
    return pl.pallas_call(
        _attn_b_sample_kernel,
        grid=(nb,),
        in_specs=[
            pl.BlockSpec((1, npair, LANES), per_b),
            pl.BlockSpec((1, 1, B_KV_WIDTH), per_b),
            pl.BlockSpec((1, 1, B_KV_WIDTH), per_b),
            pl.BlockSpec((1, WINDOW, B_KV_WIDTH), cache),
            pl.BlockSpec((1, WINDOW, B_KV_WIDTH), cache),
            pl.BlockSpec((1, npair, LANES), per_b),
            pl.BlockSpec((B_HEADS, 1), lambda b: (0, 0)),
        ],
        out_specs=[
            pl.BlockSpec((1, npair, LANES), per_b),
            pl.BlockSpec((1, WINDOW, B_KV_WIDTH), per_b),
            pl.BlockSpec((1, WINDOW, B_KV_WIDTH), per_b),
        ],
        out_shape=[
            jax.ShapeDtypeStruct((nb, npair, LANES), BF16),
            jax.ShapeDtypeStruct((nb, WINDOW, B_KV_WIDTH), F32),
            jax.ShapeDtypeStruct((nb, WINDOW, B_KV_WIDTH), F32),
        ],
        compiler_params=_cparams(1),
        name="attn_b_sample",
    )(q3, knew, vnew, ck, cv, gate3, sink_rows)


def _rope_tables(pos):
    half = B_HEAD_DIM // 2
    inv = ROPE_THETA ** (-(jnp.arange(half, dtype=F32) * 2.0 / B_HEAD_DIM))
    ang = pos.astype(F32)[:, None] * inv[None, :]
    cos, sin = jnp.cos(ang), jnp.sin(ang)
    reps = PROJ_TN // B_HEAD_DIM
    cos_t = jnp.tile(jnp.concatenate([cos, cos], axis=-1), (1, reps))
    sin_t = jnp.tile(jnp.concatenate([-sin, sin], axis=-1), (1, reps))
    return cos_t, sin_t


def _dup_heads(w):
    d = w.shape[0]
    w4 = w.reshape(d, B_KV_HEADS, 1, B_HEAD_DIM)
    return jnp.broadcast_to(w4, (d, B_KV_HEADS, 2, B_HEAD_DIM)).reshape(d, B_KV2)


def _undup_heads(x):
    lead = x.shape[:-1]
    return x.reshape(lead + (B_KV_HEADS, 2, B_HEAD_DIM))[..., 0, :]


def kernel(x_prompt, x_sample, cache_sb_k, cache_sb_v, cache_swa_k, cache_swa_v, page_table,
           norm_a, w_in_a, w_out_a, sb_bias_a, norm_b, w_in_b, w_out_b, q_norm_b, k_norm_b, sinks_b):
    assert DEC_SEQ == 1 and x_sample.shape[1] == DEC_SEQ
    bsz, seq, _ = x_prompt.shape
    nb = x_sample.shape[0]
    mp = bsz * seq
    tm_p = 512
    n_a = cache_sb_k.shape[0]
    n_pool = cache_sb_k.shape[1]

    yp = x_prompt.reshape(mp, D_MODEL)
    ys = x_sample.reshape(nb, D_MODEL)
    kcache = cache_sb_k.reshape(n_a * n_pool, PAGE_SIZE, A_KV_WIDTH)
    vcache = cache_sb_v.reshape(n_a * n_pool, PAGE_SIZE, A_KV_WIDTH)
    ck = cache_swa_k.reshape(-1, WINDOW, B_KV_WIDTH)
    cv = cache_swa_v.reshape(-1, WINDOW, B_KV_WIDTH)

    ji = jnp.arange(Q_BLOCK)
    u2 = jnp.concatenate([(ji[:, None] >= ji[None, :]), jnp.ones((Q_BLOCK, LANES), bool)],
                         axis=1).astype(BF16)
    li = jnp.arange(PROJ_TN) // B_HEAD_DIM
    seg = (li[:, None] == li[None, :]).astype(BF16)
    cos_p, sin_p = _rope_tables(jnp.arange(seq))
    cos_s, sin_s = _rope_tables(jnp.full((nb,), PAST_LEN, jnp.int32) + jnp.arange(DEC_SEQ)[0])

    sbk_p, sbv_p, sbk_s, sbv_s = [], [], [], []
    swk_p, swv_p, swk_s, swv_s = [], [], [], []
    for i in range(DEPTH):
        j = i // 2
        if i % 2 == 0:
            g = norm_a[j].reshape(1, D_MODEL)
            w_in = w_in_a[j].astype(BF16)
            w_out = w_out_a[j].astype(BF16)
            bias = sb_bias_a[j].astype(F32)
            bias_rows = jnp.broadcast_to(
                bias.reshape(A_KV_HEADS, A_GROUPS, 1, 1),
                (A_KV_HEADS, A_GROUPS, Q_BLOCK, LANES)).reshape(A_KV_HEADS, A_GROUPS * Q_BLOCK, LANES)
            bias16 = jnp.broadcast_to(bias[:, None], (A_HEADS, LANES))
            q, k, v, kb, vb, gate = _proj_a(yp, g, w_in, tm_p)
            og = _attn_a_prompt(q.reshape(bsz, seq, A_WIDTH), kb.reshape(bsz, seq, A_KV_WIDTH),
                                vb.reshape(bsz, seq, A_KV_WIDTH), gate.reshape(bsz, seq, A_WIDTH),
                                bias_rows, u2)
            yp = _outproj(og.reshape(mp, A_WIDTH), w_out, yp, tm_p)
            sbk_p.append(k.reshape(bsz, seq, A_KV_HEADS, A_HEAD_DIM))
            sbv_p.append(v.reshape(bsz, seq, A_KV_HEADS, A_HEAD_DIM))
            q, k, v, _, _, gate = _proj_a(ys, g, w_in, nb)
            og = _attn_a_sample(page_table, q.reshape(nb, A_HEADS, A_HEAD_DIM),
                                k.reshape(nb, 1, A_KV_WIDTH), v.reshape(nb, 1, A_KV_WIDTH),
                                kcache, vcache, j, bias16, u2, gate.reshape(nb, A_HEADS, A_HEAD_DIM))
            ys = _outproj(og.reshape(nb, A_WIDTH), w_out, ys, nb)
            sbk_s.append(k.reshape(nb, DEC_SEQ, A_KV_HEADS, A_HEAD_DIM))
            sbv_s.append(v.reshape(nb, DEC_SEQ, A_KV_HEADS, A_HEAD_DIM))
        else:
            g = norm_b[j].reshape(1, D_MODEL)
            wb = w_in_b[j]
            w_in = jnp.concatenate([
                wb[:, :B_WIDTH],
                _dup_heads(wb[:, B_WIDTH:B_WIDTH + B_KV_WIDTH]),
                _dup_heads(wb[:, B_WIDTH + B_KV_WIDTH:B_WIDTH + 2 * B_KV_WIDTH]),
                wb[:, B_WIDTH + 2 * B_KV_WIDTH:]], axis=1).astype(BF16)
            w_out = w_out_b[j].astype(BF16)
            reps = PROJ_TN // B_HEAD_DIM
            qg = jnp.tile(q_norm_b[j].astype(F32), reps).reshape(1, PROJ_TN)
            kg = jnp.tile(k_norm_b[j].astype(F32), reps).reshape(1, PROJ_TN)
            sinks = sinks_b[j].astype(F32)
            sink_col = jnp.broadcast_to(
                sinks.reshape(B_KV_HEADS, B_GROUPS, 1, 1),
                (B_KV_HEADS, B_GROUPS, WINDOW, 1)).reshape(B_KV_HEADS, B_GROUPS * WINDOW, 1)
            sink_rows = jnp.concatenate([sinks[0::2], sinks[1::2]]).reshape(B_HEADS, 1)
            q, k2, v2, k2b, v2b, gate = _proj_b(yp, g, w_in, seg, qg, kg, cos_p, sin_p, tm_p,
                                                 seq // tm_p)
            og = _attn_b_prompt(q.reshape(bsz, seq, B_WIDTH), k2b.reshape(bsz, seq, B_KV2),
                                v2b.reshape(bsz, seq, B_KV2), gate.reshape(bsz, seq, B_WIDTH), sink_col)
            yp = _outproj(og.reshape(mp, B_WIDTH), w_out, yp, tm_p)
            swk_p.append(_undup_heads(k2.reshape(bsz, seq, B_KV2)[:, -WINDOW:]))
            swv_p.append(_undup_heads(v2.reshape(bsz, seq, B_KV2)[:, -WINDOW:]))
            q, k2, v2, _, _, gate = _proj_b(ys, g, w_in, seg, qg, kg, cos_s, sin_s, nb, 1)
            knew = _undup_heads(k2).reshape(nb, 1, B_KV_WIDTH)
            vnew = _undup_heads(v2).reshape(nb, 1, B_KV_WIDTH)
            og, nk, nv = _attn_b_sample(q.reshape(nb, B_HEADS // 2, LANES), knew, vnew, ck, cv, j,
                                        gate.reshape(nb, B_HEADS // 2, LANES), sink_rows)
            ys = _outproj(og.reshape(nb, B_WIDTH), w_out, ys, nb)
            swk_s.append(nk.reshape(nb, WINDOW, B_KV_HEADS, B_HEAD_DIM))
            swv_s.append(nv.reshape(nb, WINDOW, B_KV_HEADS, B_HEAD_DIM))

    return (yp.reshape(bsz, seq, D_MODEL), ys.reshape(nb, DEC_SEQ, D_MODEL),
            jnp.stack(sbk_p), jnp.stack(sbv_p), jnp.stack(sbk_s), jnp.stack(sbv_s),
            jnp.stack(swk_p), jnp.stack(swv_p), jnp.stack(swk_s), jnp.stack(swv_s))
```

```python
import functools

import jax
import jax.numpy as jnp
from jax import lax
from jax.experimental import pallas as pl
from jax.experimental.pallas import tpu as pltpu

D_MODEL = 2048
DEPTH = 4
DEC_SEQ = 1
PAST_LEN = 16384
PAGE_SIZE = 128
A_HEADS = 16
A_KV_HEADS = 8
A_HEAD_DIM = 128
A_GROUPS = A_HEADS // A_KV_HEADS
A_WIDTH = A_HEADS * A_HEAD_DIM
A_KV_WIDTH = A_KV_HEADS * A_HEAD_DIM
B_HEADS = 32
B_KV_HEADS = 4
B_HEAD_DIM = 64
B_GROUPS = B_HEADS // B_KV_HEADS
B_WIDTH = B_HEADS * B_HEAD_DIM
B_KV_WIDTH = B_KV_HEADS * B_HEAD_DIM
WINDOW = 128
Q_BLOCK = 128
ROPE_THETA = 10000.0
EPS = 1e-6

LANES = 128
PROJ_TN = 512
VMEM_LIMIT = 56 * 1024 * 1024

F32 = jnp.float32
BF16 = jnp.bfloat16


def _cparams(n_axes):
    return pltpu.CompilerParams(dimension_semantics=("arbitrary",) * n_axes,
                                vmem_limit_bytes=VMEM_LIMIT)


def _nt_dot(a, b):
    return lax.dot_general(a, b, (((1,), (1,)), ((), ())), preferred_element_type=F32)


def _dot(a, b):
    return jnp.dot(a, b, preferred_element_type=F32)


def _split_dot(p, m):
    hi = p.astype(BF16)
    lo = (p - hi.astype(F32)).astype(BF16)
    return _dot(hi, m) + _dot(lo, m)


def _softplus(z):
    return jnp.maximum(z, 0.0) + jnp.log(1.0 + jnp.exp(-jnp.abs(z)))


def _silu(g):
    return g / (1.0 + jnp.exp(-g))


def _rmsnorm_rows(xf, g):
    ms = jnp.mean(xf * xf, axis=-1, keepdims=True)
    return xf * lax.rsqrt(ms + EPS) * g


def _proj_a_kernel(x_ref, g_ref, w_ref, q_ref, k_ref, v_ref, kb_ref, vb_ref, gate_ref, h_scr):
    j = pl.program_id(1)

    @pl.when(j == 0)
    def _():
        h_scr[...] = _rmsnorm_rows(x_ref[...], g_ref[...]).astype(BF16)

    acc = _dot(h_scr[...], w_ref[...])
    nq = A_WIDTH // PROJ_TN
    nk = A_KV_WIDTH // PROJ_TN

    @pl.when(j < nq)
    def _():
        q_ref[...] = (acc * (A_HEAD_DIM ** -0.5)).astype(BF16)

    @pl.when((j >= nq) & (j < nq + nk))
    def _():
        k_ref[...] = acc
        kb_ref[...] = acc.astype(BF16)

    @pl.when((j >= nq + nk) & (j < nq + 2 * nk))
    def _():
        v_ref[...] = acc
        vb_ref[...] = acc.astype(BF16)

    @pl.when(j >= nq + 2 * nk)
    def _():
        gate_ref[...] = acc


def _proj_a(x2d, g, w_bf16, tm):
    m = x2d.shape[0]
    nq = A_WIDTH // PROJ_TN
    nk = A_KV_WIDTH // PROJ_TN
    n_tiles = (2 * A_WIDTH + 2 * A_KV_WIDTH) // PROJ_TN
    grid = (m // tm, n_tiles)

    def clamp(j, lo, n):
        return jnp.clip(j - lo, 0, n - 1)

    return pl.pallas_call(
        _proj_a_kernel,
        grid=grid,
        in_specs=[
            pl.BlockSpec((tm, D_MODEL), lambda i, j: (i, 0)),
            pl.BlockSpec((1, D_MODEL), lambda i, j: (0, 0)),
            pl.BlockSpec((D_MODEL, PROJ_TN), lambda i, j: (0, j)),
        ],
        out_specs=[
            pl.BlockSpec((tm, PROJ_TN), lambda i, j: (i, clamp(j, 0, nq))),
            pl.BlockSpec((tm, PROJ_TN), lambda i, j: (i, clamp(j, nq, nk))),
            pl.BlockSpec((tm, PROJ_TN), lambda i, j: (i, clamp(j, nq + nk, nk))),
            pl.BlockSpec((tm, PROJ_TN), lambda i, j: (i, clamp(j, nq, nk))),
            pl.BlockSpec((tm, PROJ_TN), lambda i, j: (i, clamp(j, nq + nk, nk))),
            pl.BlockSpec((tm, PROJ_TN), lambda i, j: (i, clamp(j, nq + 2 * nk, nq))),
        ],
        out_shape=[
            jax.ShapeDtypeStruct((m, A_WIDTH), BF16),
            jax.ShapeDtypeStruct((m, A_KV_WIDTH), F32),
            jax.ShapeDtypeStruct((m, A_KV_WIDTH), F32),
            jax.ShapeDtypeStruct((m, A_KV_WIDTH), BF16),
            jax.ShapeDtypeStruct((m, A_KV_WIDTH), BF16),
            jax.ShapeDtypeStruct((m, A_WIDTH), F32),
        ],
        scratch_shapes=[pltpu.VMEM((tm, D_MODEL), BF16)],
        compiler_params=_cparams(2),
        name="proj_a",
    )(x2d, g, w_bf16)


B_KV2 = 2 * B_KV_WIDTH


def _head_norm_rope(acc, seg, gain, cos, sin):
    ms = _split_dot(acc * acc, seg) * (1.0 / B_HEAD_DIM)
    xn = acc * lax.rsqrt(ms + EPS) * gain
    half = B_HEAD_DIM // 2
    lane = lax.broadcasted_iota(jnp.int32, (1, LANES), 1)
    first_half = (lane % B_HEAD_DIM) < half
    partners = []
    for c in range(acc.shape[-1] // LANES):
        xc = xn[:, c * LANES:(c + 1) * LANES]
        partners.append(jnp.where(first_half, pltpu.roll(xc, LANES - half, 1), pltpu.roll(xc, half, 1)))
    return xn * cos + jnp.concatenate(partners, axis=1) * sin


def _proj_b_kernel(x_ref, g_ref, w_ref, seg_ref, qg_ref, kg_ref, cos_ref, sin_ref,
                   q_ref, k_ref, v_ref, kb_ref, vb_ref, gate_ref, h_scr):
    j = pl.program_id(1)

    @pl.when(j == 0)
    def _():
        h_scr[...] = _rmsnorm_rows(x_ref[...], g_ref[...]).astype(BF16)

    acc = _dot(h_scr[...], w_ref[...])
    nq = B_WIDTH // PROJ_TN

    @pl.when(j < nq)
    def _():
        qr = _head_norm_rope(acc, seg_ref[...], qg_ref[...], cos_ref[...], sin_ref[...])
        q_ref[...] = (qr * (B_HEAD_DIM ** -0.5)).astype(BF16)

    @pl.when(j == nq)
    def _():
        kr = _head_norm_rope(acc, seg_ref[...], kg_ref[...], cos_ref[...], sin_ref[...])
        k_ref[...] = kr
        kb_ref[...] = kr.astype(BF16)

    @pl.when(j == nq + 1)
    def _():
        v_ref[...] = acc
        vb_ref[...] = acc.astype(BF16)

    @pl.when(j > nq + 1)
    def _():
        gate_ref[...] = acc


def _proj_b(x2d, g, w_bf16, seg, qg, kg, cos_t, sin_t, tm, pos_blocks):
    m = x2d.shape[0]
    assert B_KV2 == PROJ_TN
    nq = B_WIDTH // PROJ_TN
    n_tiles = (2 * B_WIDTH + 2 * B_KV2) // PROJ_TN
    grid = (m // tm, n_tiles)

    def clamp(j, lo, n):
        return jnp.clip(j - lo, 0, n - 1)

    const = lambda i, j: (0, 0)
    return pl.pallas_call(
        _proj_b_kernel,
        grid=grid,
        in_specs=[
            pl.BlockSpec((tm, D_MODEL), lambda i, j: (i, 0)),
            pl.BlockSpec((1, D_MODEL), const),
            pl.BlockSpec((D_MODEL, PROJ_TN), lambda i, j: (0, j)),
            pl.BlockSpec((PROJ_TN, PROJ_TN), const),
            pl.BlockSpec((1, PROJ_TN), const),
            pl.BlockSpec((1, PROJ_TN), const),
            pl.BlockSpec((tm, PROJ_TN), lambda i, j: (i % pos_blocks, 0)),
            pl.BlockSpec((tm, PROJ_TN), lambda i, j: (i % pos_blocks, 0)),
        ],
        out_specs=[
            pl.BlockSpec((tm, PROJ_TN), lambda i, j: (i, clamp(j, 0, nq))),
            pl.BlockSpec((tm, PROJ_TN), lambda i, j: (i, 0)),
            pl.BlockSpec((tm, PROJ_TN), lambda i, j: (i, 0)),
            pl.BlockSpec((tm, PROJ_TN), lambda i, j: (i, 0)),
            pl.BlockSpec((tm, PROJ_TN), lambda i, j: (i, 0)),
            pl.BlockSpec((tm, PROJ_TN), lambda i, j: (i, clamp(j, nq + 2, nq))),
        ],
        out_shape=[
            jax.ShapeDtypeStruct((m, B_WIDTH), BF16),
            jax.ShapeDtypeStruct((m, B_KV2), F32),
            jax.ShapeDtypeStruct((m, B_KV2), F32),
            jax.ShapeDtypeStruct((m, B_KV2), BF16),
            jax.ShapeDtypeStruct((m, B_KV2), BF16),
            jax.ShapeDtypeStruct((m, B_WIDTH), F32),
        ],
        scratch_shapes=[pltpu.VMEM((tm, D_MODEL), BF16)],
        compiler_params=_cparams(2),
        name="proj_b",
    )(x2d, g, w_bf16, seg, qg, kg, cos_t, sin_t)


def _outproj_kernel(og_ref, w_ref, x_ref, y_ref):
    y_ref[...] = x_ref[...] + _dot(og_ref[...], w_ref[...])


def _outproj(og, w_bf16, x2d, tm, tn=1024):
    m, kdim = og.shape
    grid = (m // tm, D_MODEL // tn)
    return pl.pallas_call(
        _outproj_kernel,
        grid=grid,
        in_specs=[
            pl.BlockSpec((tm, kdim), lambda i, j: (i, 0)),
            pl.BlockSpec((kdim, tn), lambda i, j: (0, j)),
            pl.BlockSpec((tm, tn), lambda i, j: (i, j)),
        ],
        out_specs=pl.BlockSpec((tm, tn), lambda i, j: (i, j)),
        out_shape=jax.ShapeDtypeStruct((m, D_MODEL), F32),
        compiler_params=_cparams(2),
        name="outproj",
    )(og, w_bf16, x2d)


A_Q_SUPER = 4 * Q_BLOCK


def _attn_a_prompt_kernel(q_ref, k_ref, v_ref, gate_ref, bias_ref, u_ref, og_ref, out_scr, acc_scr):
    qs = pl.program_id(2)
    hd = A_HEAD_DIM
    rows = A_GROUPS * A_Q_SUPER
    per_super = A_Q_SUPER // Q_BLOCK
    q2 = jnp.concatenate([q_ref[0, :, g * hd:(g + 1) * hd] for g in range(A_GROUPS)], axis=0)
    row = lax.broadcasted_iota(jnp.int32, (rows, LANES), 0)
    bias = jnp.where(row < A_Q_SUPER, bias_ref[0, 0:1, :], bias_ref[0, 1:2, :])
    t_pos = qs * A_Q_SUPER + row % A_Q_SUPER
    lane = lax.broadcasted_iota(jnp.int32, (rows, LANES), 1)
    out_scr[...] = jnp.zeros_like(out_scr)
    acc_scr[...] = jnp.zeros_like(acc_scr)

    def block(j, masked):
        off = pl.multiple_of(j * Q_BLOCK, Q_BLOCK)
        kj = k_ref[0, pl.ds(off, Q_BLOCK), :]
        vj = v_ref[0, pl.ds(off, Q_BLOCK), :]
        z = _nt_dot(q2, kj) + bias
        p = _softplus(z)
        if masked:
            visible = (off + lane) < t_pos
            p = jnp.where(visible, p, 0.0)
        c2 = _split_dot(p, u_ref[...])
        a = jnp.exp(z - (acc_scr[...] + c2[:, :LANES]))
        if masked:
            a = jnp.where(visible, a, 0.0)
        out_scr[...] += _dot(a.astype(BF16), vj)
        acc_scr[...] += c2[:, LANES:]

    last = (qs + 1) * per_super - 1

    def masked_body(d, carry):
        block(last - d, True)
        return carry

    def full_body(d, carry):
        block(qs * per_super - 1 - d, False)
        return carry

    lax.fori_loop(0, per_super, masked_body, 0)
    lax.fori_loop(0, qs * per_super, full_body, 0)
    out = out_scr[...]
    o = jnp.concatenate([out[g * A_Q_SUPER:(g + 1) * A_Q_SUPER] for g in range(A_GROUPS)], axis=1)
    og_ref[0] = (o * _silu(gate_ref[0])).astype(BF16)


def _attn_a_prompt(q, kb, vb, gate, bias2, u2):
    bsz, seq, _ = q.shape
    gw = A_GROUPS * A_HEAD_DIM
    rows = A_GROUPS * A_Q_SUPER
    return pl.pallas_call(
        _attn_a_prompt_kernel,
        grid=(bsz, A_KV_HEADS, seq // A_Q_SUPER),
        in_specs=[
            pl.BlockSpec((1, A_Q_SUPER, gw), lambda b, h, i: (b, i, h)),
            pl.BlockSpec((1, seq, A_HEAD_DIM), lambda b, h, i: (b, 0, h)),
            pl.BlockSpec((1, seq, A_HEAD_DIM), lambda b, h, i: (b, 0, h)),
            pl.BlockSpec((1, A_Q_SUPER, gw), lambda b, h, i: (b, i, h)),
            pl.BlockSpec((1, A_GROUPS, LANES), lambda b, h, i: (h, 0, 0)),
            pl.BlockSpec((Q_BLOCK, 2 * LANES), lambda b, h, i: (0, 0)),
        ],
        out_specs=pl.BlockSpec((1, A_Q_SUPER, gw), lambda b, h, i: (b, i, h)),
        out_shape=jax.ShapeDtypeStruct((bsz, seq, A_WIDTH), BF16),
        scratch_shapes=[pltpu.VMEM((rows, A_HEAD_DIM), F32), pltpu.VMEM((rows, LANES), F32)],
        compiler_params=_cparams(3),
        name="attn_a_prompt",
    )(q, kb, vb, gate, bias2, u2)


SAMPLE_PAGES_PER_STEP = 4


def _page_rows(ref):
    return jnp.concatenate(
        [ref[pl.ds(h, PAGE_SIZE, stride=A_KV_HEADS), :].astype(BF16) for h in range(A_KV_HEADS)], axis=1)


def _attn_a_sample_kernel(pt_ref, q_ref, knew_ref, vnew_ref, *rest):
    del pt_ref
    npg = SAMPLE_PAGES_PER_STEP
    k_refs, v_refs = rest[:npg], rest[npg:2 * npg]
    bias_ref, u_ref, gate_ref, og_ref, qbd_scr, out_scr, acc_scr = rest[2 * npg:]
    j = pl.program_id(1)
    nh = A_HEADS
    row_kv = lax.broadcasted_iota(jnp.int32, (nh, A_KV_WIDTH), 0) // A_GROUPS
    col_kv = lax.broadcasted_iota(jnp.int32, (nh, A_KV_WIDTH), 1) // A_HEAD_DIM
    bias = bias_ref[...]

    @pl.when(j == 0)
    def _():
        q16 = q_ref[0].astype(F32)
        qbd = jnp.where(row_kv == col_kv, jnp.concatenate([q16] * A_KV_HEADS, axis=1), 0.0)
        qbd_scr[...] = qbd.astype(BF16)
        qb = qbd.astype(BF16).astype(F32)
        kn = knew_ref[0].astype(BF16).astype(F32)
        vn = vnew_ref[0].astype(BF16).astype(F32)
        z = jnp.sum(qb * kn, axis=-1, keepdims=True) + bias[:, :1]
        t_new = DEC_SEQ - 1
        visible = lax.broadcasted_iota(jnp.int32, (nh, DEC_SEQ), 1) < t_new
        sp = _softplus(z)
        p = jnp.where(visible, sp, 0.0)
        a = jnp.where(visible, jnp.exp(z - sp), 0.0)
        out_scr[...] = a * vn
        acc_scr[...] = jnp.broadcast_to(p, (nh, LANES))

    qbd = qbd_scr[...]
    u2 = u_ref[...]
    out = out_scr[...]
    accb = acc_scr[...]
    for r in range(npg):
        kp = _page_rows(k_refs[r])
        vp = _page_rows(v_refs[r])
        z = _nt_dot(qbd, kp) + bias
        p = _softplus(z)
        c2 = _split_dot(p, u2)
        a = jnp.exp(z - (accb + c2[:, :LANES]))
        out = out + _dot(a.astype(BF16), vp)
        accb = accb + c2[:, LANES:]
    out_scr[...] = out
    acc_scr[...] = accb

    @pl.when(j == pl.num_programs(1) - 1)
    def _():
        full = jnp.where(row_kv == col_kv, out_scr[...], 0.0)
        o = full[:, :A_HEAD_DIM]
        for h in range(1, A_KV_HEADS):
            o = o + full[:, h * A_HEAD_DIM:(h + 1) * A_HEAD_DIM]
        og_ref[0] = (o * _silu(gate_ref[0])).astype(BF16)


def _attn_a_sample(page_table, q3, knew, vnew, kcache, vcache, layer, bias16, u2, gate3):
    nb, n_pages = page_table.shape
    npg = SAMPLE_PAGES_PER_STEP
    page_rows = PAGE_SIZE * A_KV_HEADS
    n_pool = kcache.shape[0] // (page_rows * ((DEPTH + 1) // 2))
    pt_flat = page_table.reshape(-1)

    def page_idx(r):
        def index_map(b, j, pt):
            return (layer * n_pool + pt[b * n_pages + (n_pages - 1 - (j * npg + r))], 0)
        return index_map

    page_specs = [pl.BlockSpec((page_rows, A_HEAD_DIM), page_idx(r)) for r in range(npg)]
    per_b = lambda b, j, pt: (b, 0, 0)
    const = lambda b, j, pt: (0, 0)
    grid_spec = pltpu.PrefetchScalarGridSpec(
        num_scalar_prefetch=1,
        grid=(nb, n_pages // npg),
        in_specs=[
            pl.BlockSpec((1, A_HEADS, A_HEAD_DIM), per_b),
            pl.BlockSpec((1, 1, A_KV_WIDTH), per_b),
            pl.BlockSpec((1, 1, A_KV_WIDTH), per_b),
            *page_specs,
            *page_specs,
            pl.BlockSpec((A_HEADS, LANES), const),
            pl.BlockSpec((PAGE_SIZE, 2 * LANES), const),
            pl.BlockSpec((1, A_HEADS, A_HEAD_DIM), per_b),
        ],
        out_specs=pl.BlockSpec((1, A_HEADS, A_HEAD_DIM), per_b),
        scratch_shapes=[
            pltpu.VMEM((A_HEADS, A_KV_WIDTH), BF16),
            pltpu.VMEM((A_HEADS, A_KV_WIDTH), F32),
            pltpu.VMEM((A_HEADS, LANES), F32),
        ],
    )
    return pl.pallas_call(
        _attn_a_sample_kernel,
        grid_spec=grid_spec,
        out_shape=jax.ShapeDtypeStruct((nb, A_HEADS, A_HEAD_DIM), BF16),
        compiler_params=_cparams(2),
        name="attn_a_sample",
    )(pt_flat, q3, knew, vnew, *([kcache] * npg), *([vcache] * npg), bias16, u2, gate3)


def _sink_softmax(s, sink):
    m = jnp.maximum(jnp.max(s, axis=-1, keepdims=True), sink)
    e = jnp.exp(s - m)
    return e / (jnp.sum(e, axis=-1, keepdims=True) + jnp.exp(sink - m))


def _attn_b_prompt_kernel(q_ref, kp_ref, kc_ref, vp_ref, vc_ref, gate_ref, sink_ref, og_ref):
    n = pl.program_id(1)
    pairs = B_GROUPS // 2
    lane = lax.broadcasted_iota(jnp.int32, (1, LANES), 1)
    low = lane < B_HEAD_DIM
    qt = q_ref[0]
    zero = jnp.zeros((), qt.dtype)
    rows = []
    for c in range(pairs):
        qp = qt[:, c * LANES:(c + 1) * LANES]
        rows += [jnp.where(low, qp, zero), jnp.where(low, zero, qp)]
    q8 = jnp.concatenate(rows, axis=0)
    k2 = jnp.concatenate([kp_ref[0], kc_ref[0]], axis=0)
    v2 = jnp.concatenate([vp_ref[0], vc_ref[0]], axis=0)
    s = _nt_dot(q8, k2)
    nrows = B_GROUPS * WINDOW
    i = lax.broadcasted_iota(jnp.int32, (nrows, 2 * WINDOW), 0) % WINDOW
    jj = lax.broadcasted_iota(jnp.int32, (nrows, 2 * WINDOW), 1)
    rel = i + WINDOW - jj
    valid = (rel >= 0) & (rel < WINDOW) & ((n > 0) | (jj >= WINDOW))
    p = _sink_softmax(jnp.where(valid, s, -jnp.inf), sink_ref[0]).astype(BF16)
    v_lo = jnp.where(low, v2, zero)
    v_hi = jnp.where(low, zero, v2)
    outs = []
    for c in range(pairs):
        pa = p[(2 * c) * WINDOW:(2 * c + 1) * WINDOW]
        pb = p[(2 * c + 1) * WINDOW:(2 * c + 2) * WINDOW]
        outs.append(_dot(pa, v_lo) + _dot(pb, v_hi))
    o = jnp.concatenate(outs, axis=1)
    og_ref[0] = (o * _silu(gate_ref[0])).astype(BF16)


def _attn_b_prompt(q, k2b, v2b, gate, sink_col):
    bsz, seq, _ = q.shape
    nb = seq // WINDOW
    gw = B_GROUPS * B_HEAD_DIM
    prev = lambda b, n, h: (b, jnp.maximum(n - 1, 0), h)
    cur = lambda b, n, h: (b, n, h)
    return pl.pallas_call(
        _attn_b_prompt_kernel,
        grid=(bsz, nb, B_KV_HEADS),
        in_specs=[
            pl.BlockSpec((1, WINDOW, gw), cur),
            pl.BlockSpec((1, WINDOW, LANES), prev),
            pl.BlockSpec((1, WINDOW, LANES), cur),
            pl.BlockSpec((1, WINDOW, LANES), prev),
            pl.BlockSpec((1, WINDOW, LANES), cur),
            pl.BlockSpec((1, WINDOW, gw), cur),
            pl.BlockSpec((1, B_GROUPS * WINDOW, 1), lambda b, n, h: (h, 0, 0)),
        ],
        out_specs=pl.BlockSpec((1, WINDOW, gw), cur),
        out_shape=jax.ShapeDtypeStruct((bsz, seq, B_WIDTH), BF16),
        compiler_params=_cparams(3),
        name="attn_b_prompt",
    )(q, k2b, k2b, v2b, v2b, gate, sink_col)


def _attn_b_sample_kernel(q_ref, knew_ref, vnew_ref, ck_ref, cv_ref, gate_ref, sink_ref,
                          og_ref, nk_ref, nv_ref):
    win = WINDOW
    row = lax.broadcasted_iota(jnp.int32, (win, B_KV_WIDTH), 0)
    newk = jnp.where(row == win - 1, knew_ref[0], pltpu.roll(ck_ref[0], win - 1, 0))
    newv = jnp.where(row == win - 1, vnew_ref[0], pltpu.roll(cv_ref[0], win - 1, 0))
    nk_ref[0] = newk
    nv_ref[0] = newv

    npair = B_HEADS // 2
    lane = lax.broadcasted_iota(jnp.int32, (1, LANES), 1)
    low = lane < B_HEAD_DIM
    q16 = q_ref[0].astype(F32)
    qsw = pltpu.roll(q16, B_HEAD_DIM, 1)
    qa = jnp.where(low, q16, qsw)
    qb = jnp.where(low, qsw, q16)
    q32 = jnp.concatenate([jnp.concatenate([qa, qa], axis=1),
                           jnp.concatenate([qb, qb], axis=1)], axis=0)
    r = lax.broadcasted_iota(jnp.int32, (B_HEADS, B_KV_WIDTH), 0)
    head = jnp.where(r < npair, 2 * r, 2 * (r - npair) + 1)
    own = (head // B_GROUPS) == (lax.broadcasted_iota(jnp.int32, (B_HEADS, B_KV_WIDTH), 1) // B_HEAD_DIM)
    qbd = jnp.where(own, q32, 0.0).astype(BF16)
    s = _nt_dot(qbd, newk.astype(BF16))
    p = _sink_softmax(s, sink_ref[...])
    full = jnp.where(own, _dot(p.astype(BF16), newv.astype(BF16)), 0.0)
    o128 = full[:, :LANES] + full[:, LANES:]
    both = o128 + pltpu.roll(o128, B_HEAD_DIM, 1)
    o16 = jnp.where(low, both[:npair], both[npair:])
    og_ref[0] = (o16 * _silu(gate_ref[0])).astype(BF16)


def _attn_b_sample(q3, knew, vnew, ck, cv, layer, gate3, sink_rows):
    nb = q3.shape[0]
    npair = B_HEADS // 2
    per_b = lambda b: (b, 0, 0)
    cache = lambda b: (layer * nb + b, 0, 0)
    return pl.pallas_call(
        _attn_b_sample_kernel,
        grid=(nb,),
        in_specs=[
            pl.BlockSpec((1, npair, LANES), per_b),
            pl.BlockSpec((1, 1, B_KV_WIDTH), per_b),
            pl.BlockSpec((1, 1, B_KV_WIDTH), per_b),
            pl.BlockSpec((1, WINDOW, B_KV_WIDTH), cache),
            pl.BlockSpec((1, WINDOW, B_KV_WIDTH), cache),
            pl.BlockSpec((1, npair, LANES), per_b),
            pl.BlockSpec((B_HEADS, 1), lambda b: (0, 0)),
        ],
        out_specs=[
            pl.BlockSpec((1, npair, LANES), per_b),
            pl.BlockSpec((1, WINDOW, B_KV_WIDTH), per_b),
            pl.BlockSpec((1, WINDOW, B_KV_WIDTH), per_b),
        ],
        out_shape=[
            jax.ShapeDtypeStruct((nb, npair, LANES), BF16),
            jax.ShapeDtypeStruct((nb, WINDOW, B_KV_WIDTH), F32),
            jax.ShapeDtypeStruct((nb, WINDOW, B_KV_WIDTH), F32),
        ],
        compiler_params=_cparams(1),
        name="attn_b_sample",
    )(q3, knew, vnew, ck, cv, gate3, sink_rows)


def _rope_tables(pos):
    half = B_HEAD_DIM // 2
    inv = ROPE_THETA ** (-(jnp.arange(half, dtype=F32) * 2.0 / B_HEAD_DIM))
    ang = pos.astype(F32)[:, None] * inv[None, :]
    cos, sin = jnp.cos(ang), jnp.sin(ang)
    reps = PROJ_TN // B_HEAD_DIM
    cos_t = jnp.tile(jnp.concatenate([cos, cos], axis=-1), (1, reps))
    sin_t = jnp.tile(jnp.concatenate([-sin, sin], axis=-1), (1, reps))
    return cos_t, sin_t


def _dup_heads(w):
    d = w.shape[0]
    w4 = w.reshape(d, B_KV_HEADS, 1, B_HEAD_DIM)
    return jnp.broadcast_to(w4, (d, B_KV_HEADS, 2, B_HEAD_DIM)).reshape(d, B_KV2)


def _undup_heads(x):
    lead = x.shape[:-1]
    return x.reshape(lead + (B_KV_HEADS, 2, B_HEAD_DIM))[..., 0, :]


def kernel(x_prompt, x_sample, cache_sb_k, cache_sb_v, cache_swa_k, cache_swa_v, page_table,
           norm_a, w_in_a, w_out_a, sb_bias_a, norm_b, w_in_b, w_out_b, q_norm_b, k_norm_b, sinks_b):
    assert DEC_SEQ == 1 and x_sample.shape[1] == DEC_SEQ
    bsz, seq, _ = x_prompt.shape
    nb = x_sample.shape[0]
    mp = bsz * seq
    tm_p = 512
    n_a = cache_sb_k.shape[0]
    n_pool = cache_sb_k.shape[1]

    yp = x_prompt.reshape(mp, D_MODEL)
    ys = x_sample.reshape(nb, D_MODEL)
    kcache = cache_sb_k.reshape(n_a * n_pool * PAGE_SIZE * A_KV_HEADS, A_HEAD_DIM)
    vcache = cache_sb_v.reshape(n_a * n_pool * PAGE_SIZE * A_KV_HEADS, A_HEAD_DIM)
    ck = cache_swa_k.reshape(-1, WINDOW, B_KV_WIDTH)
    cv = cache_swa_v.reshape(-1, WINDOW, B_KV_WIDTH)

    ji = jnp.arange(Q_BLOCK)
    u2 = jnp.concatenate([(ji[:, None] >= ji[None, :]), jnp.ones((Q_BLOCK, LANES), bool)],
                         axis=1).astype(BF16)
    li = jnp.arange(PROJ_TN) // B_HEAD_DIM
    seg = (li[:, None] == li[None, :]).astype(BF16)
    cos_p, sin_p = _rope_tables(jnp.arange(seq))
    cos_s, sin_s = _rope_tables(jnp.full((nb,), PAST_LEN, jnp.int32) + jnp.arange(DEC_SEQ)[0])

    sbk_p, sbv_p, sbk_s, sbv_s = [], [], [], []
    swk_p, swv_p, swk_s, swv_s = [], [], [], []
    for i in range(DEPTH):
        j = i // 2
        if i % 2 == 0:
            g = norm_a[j].reshape(1, D_MODEL)
            w_in = w_in_a[j].astype(BF16)
            w_out = w_out_a[j].astype(BF16)
            bias = sb_bias_a[j].astype(F32)
            bias16 = jnp.broadcast_to(bias[:, None], (A_HEADS, LANES))
            bias2 = bias16.reshape(A_KV_HEADS, A_GROUPS, LANES)
            q, k, v, kb, vb, gate = _proj_a(yp, g, w_in, tm_p)
            og = _attn_a_prompt(q.reshape(bsz, seq, A_WIDTH), kb.reshape(bsz, seq, A_KV_WIDTH),
                                vb.reshape(bsz, seq, A_KV_WIDTH), gate.reshape(bsz, seq, A_WIDTH),
                                bias2, u2)
            yp = _outproj(og.reshape(mp, A_WIDTH), w_out, yp, tm_p)
            sbk_p.append(k.reshape(bsz, seq, A_KV_HEADS, A_HEAD_DIM))
            sbv_p.append(v.reshape(bsz, seq, A_KV_HEADS, A_HEAD_DIM))
            q, k, v, _, _, gate = _proj_a(ys, g, w_in, nb)
            og = _attn_a_sample(page_table, q.reshape(nb, A_HEADS, A_HEAD_DIM),
                                k.reshape(nb, 1, A_KV_WIDTH), v.reshape(nb, 1, A_KV_WIDTH),
                                kcache, vcache, j, bias16, u2, gate.reshape(nb, A_HEADS, A_HEAD_DIM))
            ys = _outproj(og.reshape(nb, A_WIDTH), w_out, ys, nb)
            sbk_s.append(k.reshape(nb, DEC_SEQ, A_KV_HEADS, A_HEAD_DIM))
            sbv_s.append(v.reshape(nb, DEC_SEQ, A_KV_HEADS, A_HEAD_DIM))
        else:
            g = norm_b[j].reshape(1, D_MODEL)
            wb = w_in_b[j]
            w_in = jnp.concatenate([
                wb[:, :B_WIDTH],
                _dup_heads(wb[:, B_WIDTH:B_WIDTH + B_KV_WIDTH]),
                _dup_heads(wb[:, B_WIDTH + B_KV_WIDTH:B_WIDTH + 2 * B_KV_WIDTH]),
                wb[:, B_WIDTH + 2 * B_KV_WIDTH:]], axis=1).astype(BF16)
            w_out = w_out_b[j].astype(BF16)
            reps = PROJ_TN // B_HEAD_DIM
            qg = jnp.tile(q_norm_b[j].astype(F32), reps).reshape(1, PROJ_TN)
            kg = jnp.tile(k_norm_b[j].astype(F32), reps).reshape(1, PROJ_TN)
            sinks = sinks_b[j].astype(F32)
            sink_col = jnp.broadcast_to(
                sinks.reshape(B_KV_HEADS, B_GROUPS, 1, 1),
                (B_KV_HEADS, B_GROUPS, WINDOW, 1)).reshape(B_KV_HEADS, B_GROUPS * WINDOW, 1)
            sink_rows = jnp.concatenate([sinks[0::2], sinks[1::2]]).reshape(B_HEADS, 1)
            q, k2, v2, k2b, v2b, gate = _proj_b(yp, g, w_in, seg, qg, kg, cos_p, sin_p, tm_p,
                                                 seq // tm_p)
            og = _attn_b_prompt(q.reshape(bsz, seq, B_WIDTH), k2b.reshape(bsz, seq, B_KV2),
                                v2b.reshape(bsz, seq, B_KV2), gate.reshape(bsz, seq, B_WIDTH), sink_col)
            yp = _outproj(og.reshape(mp, B_WIDTH), w_out, yp, tm_p)
            swk_p.append(_undup_heads(k2.reshape(bsz, seq, B_KV2)[:, -WINDOW:]))
            swv_p.append(_undup_heads(v2.reshape(bsz, seq, B_KV2)[:, -WINDOW:]))
            q, k2, v2, _, _, gate = _proj_b(ys, g, w_in, seg, qg, kg, cos_s, sin_s, nb, 1)
            knew = _undup_heads(k2).reshape(nb, 1, B_KV_WIDTH)
            vnew = _undup_heads(v2).reshape(nb, 1, B_KV_WIDTH)
            og, nk, nv = _attn_b_sample(q.reshape(nb, B_HEADS // 2, LANES), knew, vnew, ck, cv, j,
                                        gate.reshape(nb, B_HEADS // 2, LANES), sink_rows)
            ys = _outproj(og.reshape(nb, B_WIDTH), w_out, ys, nb)
            swk_s.append(nk.reshape(nb, WINDOW, B_KV_HEADS, B_HEAD_DIM))
            swv_s.append(nv.reshape(nb, WINDOW, B_KV_HEADS, B_HEAD_DIM))

    return (yp.reshape(bsz, seq, D_MODEL), ys.reshape(nb, DEC_SEQ, D_MODEL),
            jnp.stack(sbk_p), jnp.stack(sbv_p), jnp.stack(sbk_s), jnp.stack(sbv_s),
            jnp.stack(swk_p), jnp.stack(swv_p), jnp.stack(swk_s), jnp.stack(swv_s))
```

```python
import functools

import jax
import jax.numpy as jnp
from jax import lax
from jax.experimental import pallas as pl
from jax.experimental.pallas import tpu as pltpu

D_MODEL = 2048
DEPTH = 4
DEC_SEQ = 1
PAST_LEN = 16384
PAGE_SIZE = 128
A_HEADS = 16
A_KV_HEADS = 8
A_HEAD_DIM = 128
A_GROUPS = A_HEADS // A_KV_HEADS
A_WIDTH = A_HEADS * A_HEAD_DIM
A_KV_WIDTH = A_KV_HEADS * A_HEAD_DIM
B_HEADS = 32
B_KV_HEADS = 4
B_HEAD_DIM = 64
B_GROUPS = B_HEADS // B_KV_HEADS
B_WIDTH = B_HEADS * B_HEAD_DIM
B_KV_WIDTH = B_KV_HEADS * B_HEAD_DIM
WINDOW = 128
Q_BLOCK = 128
ROPE_THETA = 10000.0
EPS = 1e-6

LANES = 128
PROJ_TN = 512
VMEM_LIMIT = 56 * 1024 * 1024

F32 = jnp.float32
BF16 = jnp.bfloat16


def _cparams(n_axes):
    return pltpu.CompilerParams(dimension_semantics=("arbitrary",) * n_axes,
                                vmem_limit_bytes=VMEM_LIMIT)


def _nt_dot(a, b):
    return lax.dot_general(a, b, (((1,), (1,)), ((), ())), preferred_element_type=F32)


def _dot(a, b):
    return jnp.dot(a, b, preferred_element_type=F32)


def _split_dot(p, m):
    hi = p.astype(BF16)
    lo = (p - hi.astype(F32)).astype(BF16)
    return _dot(hi, m) + _dot(lo, m)


def _softplus(z):
    return jnp.maximum(z, 0.0) + jnp.log(1.0 + jnp.exp(-jnp.abs(z)))


def _silu(g):
    return g / (1.0 + jnp.exp(-g))


def _rmsnorm_rows(xf, g):
    ms = jnp.mean(xf * xf, axis=-1, keepdims=True)
    return xf * lax.rsqrt(ms + EPS) * g


def _proj_a_kernel(x_ref, g_ref, w_ref, q_ref, k_ref, v_ref, kb_ref, vb_ref, gate_ref, h_scr):
    j = pl.program_id(1)

    @pl.when(j == 0)
    def _():
        h_scr[...] = _rmsnorm_rows(x_ref[...], g_ref[...]).astype(BF16)

    acc = _dot(h_scr[...], w_ref[...])
    nq = A_WIDTH // PROJ_TN
    nk = A_KV_WIDTH // PROJ_TN

    @pl.when(j < nq)
    def _():
        q_ref[...] = (acc * (A_HEAD_DIM ** -0.5)).astype(BF16)

    @pl.when((j >= nq) & (j < nq + nk))
    def _():
        k_ref[...] = acc
        kb_ref[...] = acc.astype(BF16)

    @pl.when((j >= nq + nk) & (j < nq + 2 * nk))
    def _():
        v_ref[...] = acc
        vb_ref[...] = acc.astype(BF16)

    @pl.when(j >= nq + 2 * nk)
    def _():
        gate_ref[...] = acc


def _proj_a(x2d, g, w_bf16, tm):
    m = x2d.shape[0]
    nq = A_WIDTH // PROJ_TN
    nk = A_KV_WIDTH // PROJ_TN
    n_tiles = (2 * A_WIDTH + 2 * A_KV_WIDTH) // PROJ_TN
    grid = (m // tm, n_tiles)

    def clamp(j, lo, n):
        return jnp.clip(j - lo, 0, n - 1)

    return pl.pallas_call(
        _proj_a_kernel,
        grid=grid,
        in_specs=[
            pl.BlockSpec((tm, D_MODEL), lambda i, j: (i, 0)),
            pl.BlockSpec((1, D_MODEL), lambda i, j: (0, 0)),
            pl.BlockSpec((D_MODEL, PROJ_TN), lambda i, j: (0, j)),
        ],
        out_specs=[
            pl.BlockSpec((tm, PROJ_TN), lambda i, j: (i, clamp(j, 0, nq))),
            pl.BlockSpec((tm, PROJ_TN), lambda i, j: (i, clamp(j, nq, nk))),
            pl.BlockSpec((tm, PROJ_TN), lambda i, j: (i, clamp(j, nq + nk, nk))),
            pl.BlockSpec((tm, PROJ_TN), lambda i, j: (i, clamp(j, nq, nk))),
            pl.BlockSpec((tm, PROJ_TN), lambda i, j: (i, clamp(j, nq + nk, nk))),
            pl.BlockSpec((tm, PROJ_TN), lambda i, j: (i, clamp(j, nq + 2 * nk, nq))),
        ],
        out_shape=[
            jax.ShapeDtypeStruct((m, A_WIDTH), BF16),
            jax.ShapeDtypeStruct((m, A_KV_WIDTH), F32),
            jax.ShapeDtypeStruct((m, A_KV_WIDTH), F32),
            jax.ShapeDtypeStruct((m, A_KV_WIDTH), BF16),
            jax.ShapeDtypeStruct((m, A_KV_WIDTH), BF16),
            jax.ShapeDtypeStruct((m, A_WIDTH), F32),
        ],
        scratch_shapes=[pltpu.VMEM((tm, D_MODEL), BF16)],
        compiler_params=_cparams(2),
        name="proj_a",
    )(x2d, g, w_bf16)


B_KV2 = 2 * B_KV_WIDTH


def _head_norm_rope(acc, seg, gain, cos, sin):
    ms = _split_dot(acc * acc, seg) * (1.0 / B_HEAD_DIM)
    xn = acc * lax.rsqrt(ms + EPS) * gain
    half = B_HEAD_DIM // 2
    lane = lax.broadcasted_iota(jnp.int32, (1, LANES), 1)
    first_half = (lane % B_HEAD_DIM) < half
    partners = []
    for c in range(acc.shape[-1] // LANES):
        xc = xn[:, c * LANES:(c + 1) * LANES]
        partners.append(jnp.where(first_half, pltpu.roll(xc, LANES - half, 1), pltpu.roll(xc, half, 1)))
    return xn * cos + jnp.concatenate(partners, axis=1) * sin


def _proj_b_kernel(x_ref, g_ref, w_ref, seg_ref, qg_ref, kg_ref, cos_ref, sin_ref,
                   q_ref, k_ref, v_ref, kb_ref, vb_ref, gate_ref, h_scr):
    j = pl.program_id(1)

    @pl.when(j == 0)
    def _():
        h_scr[...] = _rmsnorm_rows(x_ref[...], g_ref[...]).astype(BF16)

    acc = _dot(h_scr[...], w_ref[...])
    nq = B_WIDTH // PROJ_TN

    @pl.when(j < nq)
    def _():
        qr = _head_norm_rope(acc, seg_ref[...], qg_ref[...], cos_ref[...], sin_ref[...])
        q_ref[...] = (qr * (B_HEAD_DIM ** -0.5)).astype(BF16)

    @pl.when(j == nq)
    def _():
        kr = _head_norm_rope(acc, seg_ref[...], kg_ref[...], cos_ref[...], sin_ref[...])
        k_ref[...] = kr
        kb_ref[...] = kr.astype(BF16)

    @pl.when(j == nq + 1)
    def _():
        v_ref[...] = acc
        vb_ref[...] = acc.astype(BF16)

    @pl.when(j > nq + 1)
    def _():
        gate_ref[...] = acc


def _proj_b(x2d, g, w_bf16, seg, qg, kg, cos_t, sin_t, tm, pos_blocks):
    m = x2d.shape[0]
    assert B_KV2 == PROJ_TN
    nq = B_WIDTH // PROJ_TN
    n_tiles = (2 * B_WIDTH + 2 * B_KV2) // PROJ_TN
    grid = (m // tm, n_tiles)

    def clamp(j, lo, n):
        return jnp.clip(j - lo, 0, n - 1)

    const = lambda i, j: (0, 0)
    return pl.pallas_call(
        _proj_b_kernel,
        grid=grid,
        in_specs=[
            pl.BlockSpec((tm, D_MODEL), lambda i, j: (i, 0)),
            pl.BlockSpec((1, D_MODEL), const),
            pl.BlockSpec((D_MODEL, PROJ_TN), lambda i, j: (0, j)),
            pl.BlockSpec((PROJ_TN, PROJ_TN), const),
            pl.BlockSpec((1, PROJ_TN), const),
            pl.BlockSpec((1, PROJ_TN), const),
            pl.BlockSpec((tm, PROJ_TN), lambda i, j: (i % pos_blocks, 0)),
            pl.BlockSpec((tm, PROJ_TN), lambda i, j: (i % pos_blocks, 0)),
        ],
        out_specs=[
            pl.BlockSpec((tm, PROJ_TN), lambda i, j: (i, clamp(j, 0, nq))),
            pl.BlockSpec((tm, PROJ_TN), lambda i, j: (i, 0)),
            pl.BlockSpec((tm, PROJ_TN), lambda i, j: (i, 0)),
            pl.BlockSpec((tm, PROJ_TN), lambda i, j: (i, 0)),
            pl.BlockSpec((tm, PROJ_TN), lambda i, j: (i, 0)),
            pl.BlockSpec((tm, PROJ_TN), lambda i, j: (i, clamp(j, nq + 2, nq))),
        ],
        out_shape=[
            jax.ShapeDtypeStruct((m, B_WIDTH), BF16),
            jax.ShapeDtypeStruct((m, B_KV2), F32),
            jax.ShapeDtypeStruct((m, B_KV2), F32),
            jax.ShapeDtypeStruct((m, B_KV2), BF16),
            jax.ShapeDtypeStruct((m, B_KV2), BF16),
            jax.ShapeDtypeStruct((m, B_WIDTH), F32),
        ],
        scratch_shapes=[pltpu.VMEM((tm, D_MODEL), BF16)],
        compiler_params=_cparams(2),
        name="proj_b",
    )(x2d, g, w_bf16, seg, qg, kg, cos_t, sin_t)


def _outproj_kernel(og_ref, w_ref, x_ref, y_ref):
    y_ref[...] = x_ref[...] + _dot(og_ref[...], w_ref[...])


def _outproj(og, w_bf16, x2d, tm, tn=1024):
    m, kdim = og.shape
    grid = (m // tm, D_MODEL // tn)
    return pl.pallas_call(
        _outproj_kernel,
        grid=grid,
        in_specs=[
            pl.BlockSpec((tm, kdim), lambda i, j: (i, 0)),
            pl.BlockSpec((kdim, tn), lambda i, j: (0, j)),
            pl.BlockSpec((tm, tn), lambda i, j: (i, j)),
        ],
        out_specs=pl.BlockSpec((tm, tn), lambda i, j: (i, j)),
        out_shape=jax.ShapeDtypeStruct((m, D_MODEL), F32),
        compiler_params=_cparams(2),
        name="outproj",
    )(og, w_bf16, x2d)


A_Q_SUPER = 4 * Q_BLOCK
A_KEY_BLOCK = 2 * Q_BLOCK


def _attn_a_prompt_kernel(q_ref, k_ref, v_ref, gate_ref, bias_ref, u_ref, og_ref, out_scr, acc_scr):
    qs = pl.program_id(2)
    hd = A_HEAD_DIM
    kb = A_KEY_BLOCK
    rows = A_GROUPS * A_Q_SUPER
    per_super = A_Q_SUPER // kb
    q2 = jnp.concatenate([q_ref[0, :, g * hd:(g + 1) * hd] for g in range(A_GROUPS)], axis=0)
    row = lax.broadcasted_iota(jnp.int32, (rows, kb), 0)
    bias = jnp.where(row < A_Q_SUPER, bias_ref[0, 0:1, :], bias_ref[0, 1:2, :])
    t_pos = qs * A_Q_SUPER + row % A_Q_SUPER
    lane = lax.broadcasted_iota(jnp.int32, (rows, kb), 1)
    out_scr[...] = jnp.zeros_like(out_scr)
    acc_scr[...] = jnp.zeros_like(acc_scr)

    def block(j, masked):
        off = pl.multiple_of(j * kb, kb)
        kj = k_ref[0, pl.ds(off, kb), :]
        vj = v_ref[0, pl.ds(off, kb), :]
        z = _nt_dot(q2, kj) + bias
        p = _softplus(z)
        if masked:
            visible = (off + lane) < t_pos
            p = jnp.where(visible, p, 0.0)
        c = _split_dot(p, u_ref[...])
        acc = acc_scr[...]
        a = jnp.exp(z - (jnp.concatenate([acc] * (kb // LANES), axis=1) + c))
        if masked:
            a = jnp.where(visible, a, 0.0)
        out_scr[...] += _dot(a.astype(BF16), vj)
        acc_scr[...] = acc + jnp.sum(p, axis=-1, keepdims=True)

    last = (qs + 1) * per_super - 1

    def masked_body(d, carry):
        block(last - d, True)
        return carry

    def full_body(d, carry):
        block(qs * per_super - 1 - d, False)
        return carry

    lax.fori_loop(0, per_super, masked_body, 0)
    lax.fori_loop(0, qs * per_super, full_body, 0)
    out = out_scr[...]
    o = jnp.concatenate([out[g * A_Q_SUPER:(g + 1) * A_Q_SUPER] for g in range(A_GROUPS)], axis=1)
    og_ref[0] = (o * _silu(gate_ref[0])).astype(BF16)


def _attn_a_prompt(q, kb, vb, gate, bias2, u_tri):
    bsz, seq, _ = q.shape
    gw = A_GROUPS * A_HEAD_DIM
    rows = A_GROUPS * A_Q_SUPER
    return pl.pallas_call(
        _attn_a_prompt_kernel,
        grid=(bsz, A_KV_HEADS, seq // A_Q_SUPER),
        in_specs=[
            pl.BlockSpec((1, A_Q_SUPER, gw), lambda b, h, i: (b, i, h)),
            pl.BlockSpec((1, seq, A_HEAD_DIM), lambda b, h, i: (b, 0, h)),
            pl.BlockSpec((1, seq, A_HEAD_DIM), lambda b, h, i: (b, 0, h)),
            pl.BlockSpec((1, A_Q_SUPER, gw), lambda b, h, i: (b, i, h)),
            pl.BlockSpec((1, A_GROUPS, A_KEY_BLOCK), lambda b, h, i: (h, 0, 0)),
            pl.BlockSpec((A_KEY_BLOCK, A_KEY_BLOCK), lambda b, h, i: (0, 0)),
        ],
        out_specs=pl.BlockSpec((1, A_Q_SUPER, gw), lambda b, h, i: (b, i, h)),
        out_shape=jax.ShapeDtypeStruct((bsz, seq, A_WIDTH), BF16),
        scratch_shapes=[pltpu.VMEM((rows, A_HEAD_DIM), F32), pltpu.VMEM((rows, LANES), F32)],
        compiler_params=_cparams(3),
        name="attn_a_prompt",
    )(q, kb, vb, gate, bias2, u_tri)


SAMPLE_PAGES_PER_STEP = 8


def _page_rows(ref):
    return jnp.concatenate(
        [ref[pl.ds(h, PAGE_SIZE, stride=A_KV_HEADS), :].astype(BF16) for h in range(A_KV_HEADS)], axis=1)


def _attn_a_sample_kernel(pt_ref, q_ref, knew_ref, vnew_ref, *rest):
    del pt_ref
    npg = SAMPLE_PAGES_PER_STEP
    k_refs, v_refs = rest[:npg], rest[npg:2 * npg]
    bias_ref, u_ref, gate_ref, og_ref, qbd_scr, out_scr, acc_scr = rest[2 * npg:]
    j = pl.program_id(1)
    nh = A_HEADS
    row_kv = lax.broadcasted_iota(jnp.int32, (nh, A_KV_WIDTH), 0) // A_GROUPS
    col_kv = lax.broadcasted_iota(jnp.int32, (nh, A_KV_WIDTH), 1) // A_HEAD_DIM
    bias = bias_ref[...]

    @pl.when(j == 0)
    def _():
        q16 = q_ref[0].astype(F32)
        qbd = jnp.where(row_kv == col_kv, jnp.concatenate([q16] * A_KV_HEADS, axis=1), 0.0)
        qbd_scr[...] = qbd.astype(BF16)
        qb = qbd.astype(BF16).astype(F32)
        kn = knew_ref[0].astype(BF16).astype(F32)
        vn = vnew_ref[0].astype(BF16).astype(F32)
        z = jnp.sum(qb * kn, axis=-1, keepdims=True) + bias[:, :1]
        t_new = DEC_SEQ - 1
        visible = lax.broadcasted_iota(jnp.int32, (nh, DEC_SEQ), 1) < t_new
        sp = _softplus(z)
        p = jnp.where(visible, sp, 0.0)
        a = jnp.where(visible, jnp.exp(z - sp), 0.0)
        out_scr[...] = a * vn
        acc_scr[...] = jnp.broadcast_to(p, (nh, LANES))

    qbd = qbd_scr[...]
    u2 = u_ref[...]
    z = jnp.concatenate([_nt_dot(qbd, _page_rows(k_refs[r])) for r in range(npg)], axis=0)
    z = z + jnp.concatenate([bias] * npg, axis=0)
    p = _softplus(z)
    c2 = _split_dot(p, u2)
    accb = acc_scr[...]
    newer = []
    for r in range(npg):
        newer.append(accb)
        accb = accb + c2[r * nh:(r + 1) * nh, LANES:]
    acc_scr[...] = accb
    a = jnp.exp(z - (jnp.concatenate(newer, axis=0) + c2[:, :LANES])).astype(BF16)
    out = out_scr[...]
    for r in range(npg):
        out = out + _dot(a[r * nh:(r + 1) * nh], _page_rows(v_refs[r]))
    out_scr[...] = out

    @pl.when(j == pl.num_programs(1) - 1)
    def _():
        full = jnp.where(row_kv == col_kv, out_scr[...], 0.0)
        o = full[:, :A_HEAD_DIM]
        for h in range(1, A_KV_HEADS):
            o = o + full[:, h * A_HEAD_DIM:(h + 1) * A_HEAD_DIM]
        og_ref[0] = (o * _silu(gate_ref[0])).astype(BF16)


def _attn_a_sample(page_table, q3, knew, vnew, kcache, vcache, layer, bias16, u2, gate3):
    nb, n_pages = page_table.shape
    npg = SAMPLE_PAGES_PER_STEP
    page_rows = PAGE_SIZE * A_KV_HEADS
    n_pool = kcache.shape[0] // (page_rows * ((DEPTH + 1) // 2))
    pt_flat = page_table.reshape(-1)

    def page_idx(r):
        def index_map(b, j, pt):
            return (layer * n_pool + pt[b * n_pages + (n_pages - 1 - (j * npg + r))], 0)
        return index_map

    page_specs = [pl.BlockSpec((page_rows, A_HEAD_DIM), page_idx(r)) for r in range(npg)]
    per_b = lambda b, j, pt: (b, 0, 0)
    const = lambda b, j, pt: (0, 0)
    grid_spec = pltpu.PrefetchScalarGridSpec(
        num_scalar_prefetch=1,
        grid=(nb, n_pages // npg),
        in_specs=[
            pl.BlockSpec((1, A_HEADS, A_HEAD_DIM), per_b),
            pl.BlockSpec((1, 1, A_KV_WIDTH), per_b),
            pl.BlockSpec((1, 1, A_KV_WIDTH), per_b),
            *page_specs,
            *page_specs,
            pl.BlockSpec((A_HEADS, LANES), const),
            pl.BlockSpec((PAGE_SIZE, 2 * LANES), const),
            pl.BlockSpec((1, A_HEADS, A_HEAD_DIM), per_b),
        ],
        out_specs=pl.BlockSpec((1, A_HEADS, A_HEAD_DIM), per_b),
        scratch_shapes=[
            pltpu.VMEM((A_HEADS, A_KV_WIDTH), BF16),
            pltpu.VMEM((A_HEADS, A_KV_WIDTH), F32),
            pltpu.VMEM((A_HEADS, LANES), F32),
        ],
    )
    return pl.pallas_call(
        _attn_a_sample_kernel,
        grid_spec=grid_spec,
        out_shape=jax.ShapeDtypeStruct((nb, A_HEADS, A_HEAD_DIM), BF16),
        compiler_params=_cparams(2),
        name="attn_a_sample",
    )(pt_flat, q3, knew, vnew, *([kcache] * npg), *([vcache] * npg), bias16, u2, gate3)


def _sink_softmax(s, sink):
    m = jnp.maximum(jnp.max(s, axis=-1, keepdims=True), sink)
    e = jnp.exp(s - m)
    return e / (jnp.sum(e, axis=-1, keepdims=True) + jnp.exp(sink - m))


def _attn_b_prompt_kernel(q_ref, kp_ref, kc_ref, vp_ref, vc_ref, gate_ref, sink_ref, og_ref):
    n = pl.program_id(1)
    pairs = B_GROUPS // 2
    lane = lax.broadcasted_iota(jnp.int32, (1, LANES), 1)
    low = lane < B_HEAD_DIM
    qt = q_ref[0]
    zero = jnp.zeros((), qt.dtype)
    rows = []
    for c in range(pairs):
        qp = qt[:, c * LANES:(c + 1) * LANES]
        rows += [jnp.where(low, qp, zero), jnp.where(low, zero, qp)]
    q8 = jnp.concatenate(rows, axis=0)
    k2 = jnp.concatenate([kp_ref[0], kc_ref[0]], axis=0)
    v2 = jnp.concatenate([vp_ref[0], vc_ref[0]], axis=0)
    s = _nt_dot(q8, k2)
    nrows = B_GROUPS * WINDOW
    i = lax.broadcasted_iota(jnp.int32, (nrows, 2 * WINDOW), 0) % WINDOW
    jj = lax.broadcasted_iota(jnp.int32, (nrows, 2 * WINDOW), 1)
    rel = i + WINDOW - jj
    valid = (rel >= 0) & (rel < WINDOW) & ((n > 0) | (jj >= WINDOW))
    p = _sink_softmax(jnp.where(valid, s, -jnp.inf), sink_ref[0]).astype(BF16)
    v_lo = jnp.where(low, v2, zero)
    v_hi = jnp.where(low, zero, v2)
    outs = []
    for c in range(pairs):
        pa = p[(2 * c) * WINDOW:(2 * c + 1) * WINDOW]
        pb = p[(2 * c + 1) * WINDOW:(2 * c + 2) * WINDOW]
        outs.append(_dot(pa, v_lo) + _dot(pb, v_hi))
    o = jnp.concatenate(outs, axis=1)
    og_ref[0] = (o * _silu(gate_ref[0])).astype(BF16)


def _attn_b_prompt(q, k2b, v2b, gate, sink_col):
    bsz, seq, _ = q.shape
    nb = seq // WINDOW
    gw = B_GROUPS * B_HEAD_DIM
    prev = lambda b, n, h: (b, jnp.maximum(n - 1, 0), h)
    cur = lambda b, n, h: (b, n, h)
    return pl.pallas_call(
        _attn_b_prompt_kernel,
        grid=(bsz, nb, B_KV_HEADS),
        in_specs=[
            pl.BlockSpec((1, WINDOW, gw), cur),
            pl.BlockSpec((1, WINDOW, LANES), prev),
            pl.BlockSpec((1, WINDOW, LANES), cur),
            pl.BlockSpec((1, WINDOW, LANES), prev),
            pl.BlockSpec((1, WINDOW, LANES), cur),
            pl.BlockSpec((1, WINDOW, gw), cur),
            pl.BlockSpec((1, B_GROUPS * WINDOW, 1), lambda b, n, h: (h, 0, 0)),
        ],
        out_specs=pl.BlockSpec((1, WINDOW, gw), cur),
        out_shape=jax.ShapeDtypeStruct((bsz, seq, B_WIDTH), BF16),
        compiler_params=_cparams(3),
        name="attn_b_prompt",
    )(q, k2b, k2b, v2b, v2b, gate, sink_col)


def _attn_b_sample_kernel(q_ref, knew_ref, vnew_ref, ck_ref, cv_ref, gate_ref, sink_ref,
                          og_ref, nk_ref, nv_ref):
    win = WINDOW
    row = lax.broadcasted_iota(jnp.int32, (win, B_KV_WIDTH), 0)
    newk = jnp.where(row == win - 1, knew_ref[0], pltpu.roll(ck_ref[0], win - 1, 0))
    newv = jnp.where(row == win - 1, vnew_ref[0], pltpu.roll(cv_ref[0], win - 1, 0))
    nk_ref[0] = newk
    nv_ref[0] = newv

    npair = B_HEADS // 2
    lane = lax.broadcasted_iota(jnp.int32, (1, LANES), 1)
    low = lane < B_HEAD_DIM
    q16 = q_ref[0].astype(F32)
    qsw = pltpu.roll(q16, B_HEAD_DIM, 1)
    qa = jnp.where(low, q16, qsw)
    qb = jnp.where(low, qsw, q16)
    q32 = jnp.concatenate([jnp.concatenate([qa, qa], axis=1),
                           jnp.concatenate([qb, qb], axis=1)], axis=0)
    r = lax.broadcasted_iota(jnp.int32, (B_HEADS, B_KV_WIDTH), 0)
    head = jnp.where(r < npair, 2 * r, 2 * (r - npair) + 1)
    own = (head // B_GROUPS) == (lax.broadcasted_iota(jnp.int32, (B_HEADS, B_KV_WIDTH), 1) // B_HEAD_DIM)
    qbd = jnp.where(own, q32, 0.0).astype(BF16)
    s = _nt_dot(qbd, newk.astype(BF16))
    p = _sink_softmax(s, sink_ref[...])
    full = jnp.where(own, _dot(p.astype(BF16), newv.astype(BF16)), 0.0)
    o128 = full[:, :LANES] + full[:, LANES:]
    both = o128 + pltpu.roll(o128, B_HEAD_DIM, 1)
    o16 = jnp.where(low, both[:npair], both[npair:])
    og_ref[0] = (o16 * _silu(gate_ref[0])).astype(BF16)


def _attn_b_sample(q3, knew, vnew, ck, cv, layer, gate3, sink_rows):
    nb = q3.shape[0]
    npair = B_HEADS // 2
    per_b = lambda b: (b, 0, 0)
    cache = lambda b: (layer * nb + b, 0, 0)
    return pl.pallas_call(
        _attn_b_sample_kernel,
        grid=(nb,),
        in_specs=[
            pl.BlockSpec((1, npair, LANES), per_b),
            pl.BlockSpec((1, 1, B_KV_WIDTH), per_b),
            pl.BlockSpec((1, 1, B_KV_WIDTH), per_b),
            pl.BlockSpec((1, WINDOW, B_KV_WIDTH), cache),
            pl.BlockSpec((1, WINDOW, B_KV_WIDTH), cache),
            pl.BlockSpec((1, npair, LANES), per_b),
            pl.BlockSpec((B_HEADS, 1), lambda b: (0, 0)),
        ],
        out_specs=[
            pl.BlockSpec((1, npair, LANES), per_b),
            pl.BlockSpec((1, WINDOW, B_KV_WIDTH), per_b),
            pl.BlockSpec((1, WINDOW, B_KV_WIDTH), per_b),
        ],
        out_shape=[
            jax.ShapeDtypeStruct((nb, npair, LANES), BF16),
            jax.ShapeDtypeStruct((nb, WINDOW, B_KV_WIDTH), F32),
            jax.ShapeDtypeStruct((nb, WINDOW, B_KV_WIDTH), F32),
        ],
        compiler_params=_cparams(1),
        name="attn_b_sample",
    )(q3, knew, vnew, ck, cv, gate3, sink_rows)


def _rope_tables(pos):
    half = B_HEAD_DIM // 2
    inv = ROPE_THETA ** (-(jnp.arange(half, dtype=F32) * 2.0 / B_HEAD_DIM))
    ang = pos.astype(F32)[:, None] * inv[None, :]
    cos, sin = jnp.cos(ang), jnp.sin(ang)
    reps = PROJ_TN // B_HEAD_DIM
    cos_t = jnp.tile(jnp.concatenate([cos, cos], axis=-1), (1, reps))
    sin_t = jnp.tile(jnp.concatenate([-sin, sin], axis=-1), (1, reps))
    return cos_t, sin_t


def _dup_heads(w):
    d = w.shape[0]
    w4 = w.reshape(d, B_KV_HEADS, 1, B_HEAD_DIM)
    return jnp.broadcast_to(w4, (d, B_KV_HEADS, 2, B_HEAD_DIM)).reshape(d, B_KV2)


def _undup_heads(x):
    lead = x.shape[:-1]
    return x.reshape(lead + (B_KV_HEADS, 2, B_HEAD_DIM))[..., 0, :]


def kernel(x_prompt, x_sample, cache_sb_k, cache_sb_v, cache_swa_k, cache_swa_v, page_table,
           norm_a, w_in_a, w_out_a, sb_bias_a, norm_b, w_in_b, w_out_b, q_norm_b, k_norm_b, sinks_b):
    assert DEC_SEQ == 1 and x_sample.shape[1] == DEC_SEQ
    bsz, seq, _ = x_prompt.shape
    nb = x_sample.shape[0]
    mp = bsz * seq
    tm_p = 512
    n_a = cache_sb_k.shape[0]
    n_pool = cache_sb_k.shape[1]

    yp = x_prompt.reshape(mp, D_MODEL)
    ys = x_sample.reshape(nb, D_MODEL)
    kcache = cache_sb_k.reshape(n_a * n_pool * PAGE_SIZE * A_KV_HEADS, A_HEAD_DIM)
    vcache = cache_sb_v.reshape(n_a * n_pool * PAGE_SIZE * A_KV_HEADS, A_HEAD_DIM)
    ck = cache_swa_k.reshape(-1, WINDOW, B_KV_WIDTH)
    cv = cache_swa_v.reshape(-1, WINDOW, B_KV_WIDTH)

    ji = jnp.arange(Q_BLOCK)
    u2 = jnp.concatenate([(ji[:, None] >= ji[None, :]), jnp.ones((Q_BLOCK, LANES), bool)],
                         axis=1).astype(BF16)
    jk = jnp.arange(A_KEY_BLOCK)
    u_tri = (jk[:, None] >= jk[None, :]).astype(BF16)
    li = jnp.arange(PROJ_TN) // B_HEAD_DIM
    seg = (li[:, None] == li[None, :]).astype(BF16)
    cos_p, sin_p = _rope_tables(jnp.arange(seq))
    cos_s, sin_s = _rope_tables(jnp.full((nb,), PAST_LEN, jnp.int32) + jnp.arange(DEC_SEQ)[0])

    sbk_p, sbv_p, sbk_s, sbv_s = [], [], [], []
    swk_p, swv_p, swk_s, swv_s = [], [], [], []
    for i in range(DEPTH):
        j = i // 2
        if i % 2 == 0:
            g = norm_a[j].reshape(1, D_MODEL)
            w_in = w_in_a[j].astype(BF16)
            w_out = w_out_a[j].astype(BF16)
            bias = sb_bias_a[j].astype(F32)
            bias16 = jnp.broadcast_to(bias[:, None], (A_HEADS, LANES))
            bias2 = jnp.broadcast_to(bias.reshape(A_KV_HEADS, A_GROUPS, 1),
                                     (A_KV_HEADS, A_GROUPS, A_KEY_BLOCK))
            q, k, v, kb, vb, gate = _proj_a(yp, g, w_in, tm_p)
            og = _attn_a_prompt(q.reshape(bsz, seq, A_WIDTH), kb.reshape(bsz, seq, A_KV_WIDTH),
                                vb.reshape(bsz, seq, A_KV_WIDTH), gate.reshape(bsz, seq, A_WIDTH),
                                bias2, u_tri)
            yp = _outproj(og.reshape(mp, A_WIDTH), w_out, yp, tm_p)
            sbk_p.append(k.reshape(bsz, seq, A_KV_HEADS, A_HEAD_DIM))
            sbv_p.append(v.reshape(bsz, seq, A_KV_HEADS, A_HEAD_DIM))
            q, k, v, _, _, gate = _proj_a(ys, g, w_in, nb)
            og = _attn_a_sample(page_table, q.reshape(nb, A_HEADS, A_HEAD_DIM),
                                k.reshape(nb, 1, A_KV_WIDTH), v.reshape(nb, 1, A_KV_WIDTH),
                                kcache, vcache, j, bias16, u2, gate.reshape(nb, A_HEADS, A_HEAD_DIM))
            ys = _outproj(og.reshape(nb, A_WIDTH), w_out, ys, nb)
            sbk_s.append(k.reshape(nb, DEC_SEQ, A_KV_HEADS, A_HEAD_DIM))
            sbv_s.append(v.reshape(nb, DEC_SEQ, A_KV_HEADS, A_HEAD_DIM))
        else:
            g = norm_b[j].reshape(1, D_MODEL)
            wb = w_in_b[j]
            w_in = jnp.concatenate([
                wb[:, :B_WIDTH],
                _dup_heads(wb[:, B_WIDTH:B_WIDTH + B_KV_WIDTH]),
                _dup_heads(wb[:, B_WIDTH + B_KV_WIDTH:B_WIDTH + 2 * B_KV_WIDTH]),
                wb[:, B_WIDTH + 2 * B_KV_WIDTH:]], axis=1).astype(BF16)
            w_out = w_out_b[j].astype(BF16)
            reps = PROJ_TN // B_HEAD_DIM
            qg = jnp.tile(q_norm_b[j].astype(F32), reps).reshape(1, PROJ_TN)
            kg = jnp.tile(k_norm_b[j].astype(F32), reps).reshape(1, PROJ_TN)
            sinks = sinks_b[j].astype(F32)
            sink_col = jnp.broadcast_to(
                sinks.reshape(B_KV_HEADS, B_GROUPS, 1, 1),
                (B_KV_HEADS, B_GROUPS, WINDOW, 1)).reshape(B_KV_HEADS, B_GROUPS * WINDOW, 1)
            sink_rows = jnp.concatenate([sinks[0::2], sinks[1::2]]).reshape(B_HEADS, 1)
            q, k2, v2, k2b, v2b, gate = _proj_b(yp, g, w_in, seg, qg, kg, cos_p, sin_p, tm_p,
                                                 seq // tm_p)
            og = _attn_b_prompt(q.reshape(bsz, seq, B_WIDTH), k2b.reshape(bsz, seq, B_KV2),
                                v2b.reshape(bsz, seq, B_KV2), gate.reshape(bsz, seq, B_WIDTH), sink_col)
            yp = _outproj(og.reshape(mp, B_WIDTH), w_out, yp, tm_p)
            swk_p.append(_undup_heads(k2.reshape(bsz, seq, B_KV2)[:, -WINDOW:]))
            swv_p.append(_undup_heads(v2.reshape(bsz, seq, B_KV2)[:, -WINDOW:]))
            q, k2, v2, _, _, gate = _proj_b(ys, g, w_in, seg, qg, kg, cos_s, sin_s, nb, 1)
            knew = _undup_heads(k2).reshape(nb, 1, B_KV_WIDTH)
            vnew = _undup_heads(v2).reshape(nb, 1, B_KV_WIDTH)
            og, nk, nv = _attn_b_sample(q.reshape(nb, B_HEADS // 2, LANES), knew, vnew, ck, cv, j,
                                        gate.reshape(nb, B_HEADS // 2, LANES), sink_rows)
            ys = _outproj(og.reshape(nb, B_WIDTH), w_out, ys, nb)
            swk_s.append(nk.reshape(nb, WINDOW, B_KV_HEADS, B_HEAD_DIM))
            swv_s.append(nv.reshape(nb, WINDOW, B_KV_HEADS, B_HEAD_DIM))

    return (yp.reshape(bsz, seq, D_MODEL), ys.reshape(nb, DEC_SEQ, D_MODEL),
            jnp.stack(sbk_p), jnp.stack(sbv_p), jnp.stack(sbk_s), jnp.stack(sbv_s),
            jnp.stack(swk_p), jnp.stack(swv_p), jnp.stack(swk_s), jnp.stack(swv_s))
```
